```python
import jax, jax.numpy as jnp
from jax import lax
import numpy as np

D_MODEL = 1024
BATCH = 8
SEQ = 4096
DEPTH = 1

EXPAND = 2
D_MIX = EXPAND * D_MODEL
D_CONV = D_MIX // 2
D_GMLP = D_MIX - D_CONV
CONV_GROUPS = 8
CONV_GROUP_DIM = D_CONV // CONV_GROUPS
GMLP_HEADS = 8
GMLP_HEAD_DIM = D_GMLP // GMLP_HEADS
CONV_WIDTH = 31
CONV_HALF = CONV_WIDTH // 2
CHUNK = 128
EPS = 1e-6
D_IN = 3 * D_CONV + 3 * D_GMLP
SPLITS = (D_CONV, 2 * D_CONV, 3 * D_CONV, 3 * D_CONV + D_GMLP, 3 * D_CONV + 2 * D_GMLP)

kernel_name = "hybrid_conv_gmlp_adaln_encoder"


def rms_norm(x, g):
    xf = x.astype(jnp.float32)
    xf = xf * lax.rsqrt(jnp.mean(xf * xf, axis=-1, keepdims=True) + EPS)
    return (xf * g.astype(jnp.float32)).astype(x.dtype)


def layer_norm(x, g, b):
    xf = x.astype(jnp.float32)
    mu = jnp.mean(xf, axis=-1, keepdims=True)
    var = jnp.mean(jnp.square(xf - mu), axis=-1, keepdims=True)
    y = (xf - mu) * lax.rsqrt(var + EPS) * g.astype(jnp.float32) + b.astype(jnp.float32)
    return y.astype(x.dtype)


def setup_inputs(seed: int = 0) -> dict:
    key = jax.random.key(seed)
    ks = jax.random.split(key, 16)
    L = DEPTH
    nrm = jax.random.normal
    return {
        "x": nrm(ks[0], (BATCH, SEQ, D_MODEL), jnp.float32),
        "c": nrm(ks[1], (BATCH, D_MODEL), jnp.float32),
        "w_ada": nrm(ks[2], (L, D_MODEL, 3 * D_MODEL), jnp.float32) * D_MODEL ** -0.5,
        "b_ada": nrm(ks[3], (L, 3 * D_MODEL), jnp.float32) * 0.02,
        "norm_g": 1.0 + 0.02 * nrm(ks[4], (L, D_MODEL), jnp.float32),
        "w_in": nrm(ks[5], (L, D_MODEL, D_IN), jnp.float32) * D_MODEL ** -0.5,
        "conv_w": nrm(ks[6], (L, CONV_WIDTH, 1, D_CONV), jnp.float32) * CONV_WIDTH ** -0.5,
        "conv_b": nrm(ks[7], (L, D_CONV), jnp.float32) * 0.02,
        "conv_ln_g": 1.0 + 0.02 * nrm(ks[8], (L, D_CONV), jnp.float32),
        "conv_ln_b": nrm(ks[9], (L, D_CONV), jnp.float32) * 0.02,
        "sg_ln_g": 1.0 + 0.02 * nrm(ks[10], (L, D_GMLP), jnp.float32),
        "sg_ln_b": nrm(ks[11], (L, D_GMLP), jnp.float32) * 0.02,
        "w_s": nrm(ks[12], (L, GMLP_HEADS, CHUNK, CHUNK), jnp.float32) * CHUNK ** -0.5,
        "b_s": 1.0 + 0.02 * nrm(ks[13], (L, GMLP_HEADS, CHUNK), jnp.float32),
        "w_out": nrm(ks[14], (L, D_MIX, D_MODEL), jnp.float32) * D_MIX ** -0.5,
        "final_g": 1.0 + 0.02 * nrm(ks[15], (D_MODEL,), jnp.float32),
    }


def conv_group(a, a_glu, a_gate, conv_w, conv_b, ln_g, ln_b):
    a = a * jax.nn.sigmoid(a_glu)
    a = lax.conv_general_dilated(
        a, conv_w.astype(a.dtype), window_strides=(1,),
        padding=[(CONV_HALF, CONV_HALF)],
        dimension_numbers=("NWC", "WIO", "NWC"),
        feature_group_count=D_CONV) + conv_b
    a = jax.nn.silu(layer_norm(a, ln_g, ln_b))
    return a * jax.nn.silu(a_gate)


def gmlp_group(u, v, b_gate, ln_g, ln_b, w_s, b_s):
    B, S, _ = v.shape
    v = layer_norm(v, ln_g, ln_b)
    v = v.reshape(B, S // CHUNK, CHUNK, GMLP_HEADS, GMLP_HEAD_DIM)
    v = jnp.einsum("hpq,bnqhd->bnphd", w_s.astype(v.dtype), v) \
        + jnp.transpose(b_s)[None, None, :, :, None]
    v = v.reshape(B, S, D_GMLP)
    return u * v * jax.nn.silu(b_gate)


def reference(x, c, w_ada, b_ada, norm_g, w_in, conv_w, conv_b, conv_ln_g, conv_ln_b,
              sg_ln_g, sg_ln_b, w_s, b_s, w_out, final_g):
    c_act = jax.nn.silu(c)
    for l in range(DEPTH):
        mod = jnp.einsum("bd,de->be", c_act, w_ada[l]) + b_ada[l]
        shift, scale, gate = jnp.split(mod, 3, axis=-1)
        h = rms_norm(x, norm_g[l]) * (1.0 + scale[:, None, :]) + shift[:, None, :]
        z = jnp.einsum("bsd,de->bse", h, w_in[l])
        a, a_glu, a_gate, u, v, b_gate = jnp.split(z, SPLITS, axis=-1)
        y_a = conv_group(a, a_glu, a_gate, conv_w[l], conv_b[l], conv_ln_g[l], conv_ln_b[l])
        y_b = gmlp_group(u, v, b_gate, sg_ln_g[l], sg_ln_b[l], w_s[l], b_s[l])
        y = jnp.einsum("bse,ed->bsd", jnp.concatenate([y_a, y_b], axis=-1), w_out[l])
        x = x + gate[:, None, :] * y
    return rms_norm(x, final_g)
```

```python
import functools

import jax
import jax.numpy as jnp
from jax import lax
from jax.experimental import pallas as pl
from jax.experimental.pallas import tpu as pltpu

EPS = 1e-6
CONV_WIDTH = 31
CONV_HALF = CONV_WIDTH // 2
CHUNK = 128
HEAD_DIM = 128
LANES = 128
SUBLANES = 8
HALO = 16
SEQ_TILE = 512
CONV_POS_BLOCK = 16
ROW_BLOCK = 64
VMEM_LIMIT_BYTES = 56 * 1024 * 1024


def _sigmoid(v):
    return 1.0 / (1.0 + jnp.exp(-v))


def _silu(v):
    return v * _sigmoid(v)


def _layer_norm(v, g, b):
    mu = jnp.mean(v, axis=-1, keepdims=True)
    d = v - mu
    var = jnp.mean(d * d, axis=-1, keepdims=True)
    return d * lax.rsqrt(var + EPS) * g + b


def _ada_body(c_ref, w_ref, b_ref, o_ref):
    c_act = _silu(c_ref[...])
    o_ref[...] = jnp.dot(c_act, w_ref[...], preferred_element_type=jnp.float32,
                         precision=lax.Precision.HIGHEST) + b_ref[...]


def _ada_call(c, w_ada, b_ada):
    bsz, d = c.shape
    n = w_ada.shape[1]
    bn = 512
    return pl.pallas_call(
        _ada_body,
        grid=(n // bn,),
        in_specs=[
            pl.BlockSpec((bsz, d), lambda j: (0, 0)),
            pl.BlockSpec((d, bn), lambda j: (0, j)),
            pl.BlockSpec((1, bn), lambda j: (0, j)),
        ],
        out_specs=pl.BlockSpec((bsz, bn), lambda j: (0, j)),
        out_shape=jax.ShapeDtypeStruct((bsz, n), jnp.float32),
        name="ada_mod",
    )(c, w_ada, b_ada.reshape(1, n))


def _layer_body(xp_ref, xm_ref, xn_ref, shift_ref, scale_ref, gate_ref, ng_ref,
                win_ref, cw_ref, cb_ref, clg_ref, clb_ref, slg_ref, slb_ref,
                ws_ref, bs_ref, wout_ref, fg_ref, o_ref,
                h_s, z_s, g_s, co_s, y_s, *, apply_final_norm):
    t = xm_ref.shape[1]
    d = xm_ref.shape[2]
    ext = t + 2 * HALO
    i = pl.program_id(1)
    n_tiles = pl.num_programs(1)

    mult = ng_ref[...] * (1.0 + scale_ref[0])
    shift = shift_ref[0]

    def modulate(xv):
        ms = jnp.mean(xv * xv, axis=-1, keepdims=True)
        return (xv * lax.rsqrt(ms + EPS) * mult + shift).astype(jnp.bfloat16)

    h_s[0:HALO, :] = modulate(xp_ref[0])
    h_s[HALO + t:ext, :] = modulate(xn_ref[0])

    def mod_step(r, carry):
        r0 = pl.multiple_of(r * ROW_BLOCK, ROW_BLOCK)
        h_s[pl.ds(HALO + r0, ROW_BLOCK), :] = modulate(xm_ref[0, pl.ds(r0, ROW_BLOCK), :])
        return carry
    lax.fori_loop(0, t // ROW_BLOCK, mod_step, 0)

    z_s[:, 0:2 * d] = jnp.dot(h_s[...], win_ref[:, 0:2 * d],
                              preferred_element_type=jnp.float32)
    z_s[0:t, 2 * d:6 * d] = jnp.dot(h_s[HALO:HALO + t, :], win_ref[:, 2 * d:6 * d],
                                    preferred_element_type=jnp.float32)

    n_cb = d // LANES
    for s in range(n_cb):
        a = z_s[:, s * LANES:(s + 1) * LANES]
        gl = z_s[:, d + s * LANES:d + (s + 1) * LANES]
        g_s[pl.ds(s, ext, stride=n_cb), :] = a * _sigmoid(gl)

    @pl.when(i == 0)
    def _():
        g_s[0:HALO * n_cb, :] = jnp.zeros((HALO * n_cb, LANES), jnp.float32)

    @pl.when(i == n_tiles - 1)
    def _():
        g_s[(HALO + t) * n_cb:ext * n_cb, :] = jnp.zeros((HALO * n_cb, LANES), jnp.float32)

    p = CONV_POS_BLOCK
    first = HALO - CONV_HALF

    def conv_step(j, carry):
        base = pl.multiple_of(j * (p * n_cb), p * n_cb)
        acc = jnp.broadcast_to(cb_ref[...][None], (p, n_cb, LANES))
        for k in range(CONV_WIDTH):
            gk = g_s[pl.ds(base + (first + k) * n_cb, p * n_cb), :]
            acc = acc + gk.reshape(p, n_cb, LANES) * cw_ref[k][None]
        co_s[pl.ds(base, p * n_cb), :] = acc.reshape(p * n_cb, LANES)
        return carry
    lax.fori_loop(0, t // p, conv_step, 0)

    clg = clg_ref[...]
    clb = clb_ref[...]

    def conv_tail(r, carry):
        r0 = pl.multiple_of(r * ROW_BLOCK, ROW_BLOCK)
        cv = jnp.concatenate(
            [co_s[pl.ds(r0 * n_cb + s, ROW_BLOCK, stride=n_cb), :] for s in range(n_cb)],
            axis=-1)
        ya = _silu(_layer_norm(cv, clg, clb)) * _silu(z_s[pl.ds(r0, ROW_BLOCK), 2 * d:3 * d])
        y_s[pl.ds(r0, ROW_BLOCK), 0:d] = ya.astype(jnp.bfloat16)
        return carry
    lax.fori_loop(0, t // ROW_BLOCK, conv_tail, 0)

    slg = slg_ref[...]
    slb = slb_ref[...]
    n_heads = d // HEAD_DIM

    def gmlp_step(q, carry):
        r0 = pl.multiple_of(q * CHUNK, CHUNK)
        vn = _layer_norm(z_s[pl.ds(r0, CHUNK), 4 * d:5 * d], slg, slb).astype(jnp.bfloat16)
        mixed = jnp.concatenate(
            [jnp.dot(ws_ref[hd], vn[:, hd * HEAD_DIM:(hd + 1) * HEAD_DIM],
                     preferred_element_type=jnp.float32) for hd in range(n_heads)],
            axis=-1) + bs_ref[...]
        yb = z_s[pl.ds(r0, CHUNK), 3 * d:4 * d] * mixed * _silu(z_s[pl.ds(r0, CHUNK), 5 * d:6 * d])
        y_s[pl.ds(r0, CHUNK), d:2 * d] = yb.astype(jnp.bfloat16)
        return carry
    lax.fori_loop(0, t // CHUNK, gmlp_step, 0)

    z_s[0:t, 0:d] = jnp.dot(y_s[...], wout_ref[...], preferred_element_type=jnp.float32)
    gate = gate_ref[0]
    fg = fg_ref[...]

    def out_step(r, carry):
        r0 = pl.multiple_of(r * ROW_BLOCK, ROW_BLOCK)
        xo = xm_ref[0, pl.ds(r0, ROW_BLOCK), :] + gate * z_s[pl.ds(r0, ROW_BLOCK), 0:d]
        if apply_final_norm:
            ms = jnp.mean(xo * xo, axis=-1, keepdims=True)
            xo = xo * lax.rsqrt(ms + EPS) * fg
        o_ref[0, pl.ds(r0, ROW_BLOCK), :] = xo
        return carry
    lax.fori_loop(0, t // ROW_BLOCK, out_step, 0)


def _layer_call(x, shift, scale, gate, norm_g, w_in, conv_w, conv_b, conv_ln_g, conv_ln_b,
                sg_ln_g, sg_ln_b, w_s, b_s, w_out, final_g, *, apply_final_norm):
    bsz, seq, d = x.shape
    t = SEQ_TILE
    assert seq % t == 0 and t % CHUNK == 0 and t % ROW_BLOCK == 0 and t % CONV_POS_BLOCK == 0
    assert d % LANES == 0 and d // LANES == SUBLANES and w_in.shape == (d, 6 * d)
    n_heads = w_s.shape[0]
    assert w_s.shape == (n_heads, CHUNK, CHUNK) and n_heads * HEAD_DIM == d
    ext = t + 2 * HALO
    n_cb = d // LANES
    halo_blocks_per_tile = t // HALO
    n_halo_blocks = seq // HALO

    const2 = lambda b, i: (0, 0)
    const3 = lambda b, i: (0, 0, 0)
    once = pl.Buffered(1)
    row = lambda v: v.reshape(1, d).astype(jnp.float32)

    bs_full = jnp.repeat(jnp.transpose(b_s), HEAD_DIM, axis=1).astype(jnp.float32)

    in_specs = [
        pl.BlockSpec((1, HALO, d), lambda b, i: (b, jnp.maximum(i * halo_blocks_per_tile - 1, 0), 0)),
        pl.BlockSpec((1, t, d), lambda b, i: (b, i, 0)),
        pl.BlockSpec((1, HALO, d),
                     lambda b, i: (b, jnp.minimum((i + 1) * halo_blocks_per_tile, n_halo_blocks - 1), 0)),
        pl.BlockSpec((1, 1, d), lambda b, i: (b, 0, 0)),
        pl.BlockSpec((1, 1, d), lambda b, i: (b, 0, 0)),
        pl.BlockSpec((1, 1, d), lambda b, i: (b, 0, 0)),
        pl.BlockSpec((1, d), const2, pipeline_mode=once),
        pl.BlockSpec((d, 6 * d), const2, pipeline_mode=once),
        pl.BlockSpec((CONV_WIDTH, n_cb, LANES), const3, pipeline_mode=once),
        pl.BlockSpec((n_cb, LANES), const2, pipeline_mode=once),
        pl.BlockSpec((1, d), const2, pipeline_mode=once),
        pl.BlockSpec((1, d), const2, pipeline_mode=once),
        pl.BlockSpec((1, d), const2, pipeline_mode=once),
        pl.BlockSpec((1, d), const2, pipeline_mode=once),
        pl.BlockSpec((n_heads, CHUNK, CHUNK), const3, pipeline_mode=once),
        pl.BlockSpec((CHUNK, d), const2, pipeline_mode=once),
        pl.BlockSpec((2 * d, d), const2, pipeline_mode=once),
        pl.BlockSpec((1, d), const2, pipeline_mode=once),
    ]
    scratch_shapes = [
        pltpu.VMEM((ext, d), jnp.bfloat16),
        pltpu.VMEM((ext, 6 * d), jnp.float32),
        pltpu.VMEM((ext * n_cb, LANES), jnp.float32),
        pltpu.VMEM((t * n_cb, LANES), jnp.float32),
        pltpu.VMEM((t, 2 * d), jnp.bfloat16),
    ]
    return pl.pallas_call(
        functools.partial(_layer_body, apply_final_norm=apply_final_norm),
        grid=(bsz, seq // t),
        in_specs=in_specs,
        out_specs=pl.BlockSpec((1, t, d), lambda b, i: (b, i, 0)),
        out_shape=jax.ShapeDtypeStruct((bsz, seq, d), jnp.float32),
        scratch_shapes=scratch_shapes,
        compiler_params=pltpu.CompilerParams(
            dimension_semantics=("arbitrary", "arbitrary"),
            vmem_limit_bytes=VMEM_LIMIT_BYTES),
        name="hybrid_layer",
    )(x, x, x, shift, scale, gate, row(norm_g), w_in.astype(jnp.bfloat16),
      conv_w.reshape(CONV_WIDTH, n_cb, LANES), conv_b.reshape(n_cb, LANES),
      row(conv_ln_g), row(conv_ln_b), row(sg_ln_g), row(sg_ln_b),
      w_s.astype(jnp.bfloat16), bs_full, w_out.astype(jnp.bfloat16), row(final_g))


def kernel(x, c, w_ada, b_ada, norm_g, w_in, conv_w, conv_b, conv_ln_g, conv_ln_b, sg_ln_g, sg_ln_b, w_s, b_s, w_out, final_g):
    depth = w_in.shape[0]
    bsz, _, d = x.shape
    for l in range(depth):
        mod = _ada_call(c, w_ada[l], b_ada[l])
        shift, scale, gate = (mod[:, k * d:(k + 1) * d].reshape(bsz, 1, d) for k in range(3))
        x = _layer_call(x, shift, scale, gate, norm_g[l], w_in[l], conv_w[l], conv_b[l],
                        conv_ln_g[l], conv_ln_b[l], sg_ln_g[l], sg_ln_b[l], w_s[l], b_s[l],
                        w_out[l], final_g, apply_final_norm=(l == depth - 1))
    return x
```

```python
import functools

import jax
import jax.numpy as jnp
from jax import lax
from jax.experimental import pallas as pl
from jax.experimental.pallas import tpu as pltpu

EPS = 1e-6
CONV_WIDTH = 31
CONV_HALF = CONV_WIDTH // 2
CHUNK = 128
HEAD_DIM = 128
LANES = 128
SUBLANES = 8
HALO = 16
SLOT_POS = CHUNK + 2 * HALO
CHUNKS_PER_STEP = 4
G_SLOTS = 4
PIPE_LAG = 4
CONV_POS_BLOCK = 8
CONV_TAP_GROUP = 8
MXU_COLS = 512
VMEM_LIMIT_BYTES = 56 * 1024 * 1024


def _sigmoid(v):
    return 1.0 / (1.0 + jnp.exp(-v))


def _silu(v):
    return v * _sigmoid(v)


def _layer_norm(v, g, b):
    mu = jnp.mean(v, axis=-1, keepdims=True)
    d = v - mu
    var = jnp.mean(d * d, axis=-1, keepdims=True)
    return d * lax.rsqrt(var + EPS) * g + b


def _ada_body(c_ref, w_ref, b_ref, o_ref):
    c_act = _silu(c_ref[...])
    o_ref[...] = jnp.dot(c_act, w_ref[...], preferred_element_type=jnp.float32,
                         precision=lax.Precision.HIGHEST) + b_ref[...]


def _ada_call(c, w_ada, b_ada):
    bsz, d = c.shape
    n = w_ada.shape[1]
    bn = 512
    return pl.pallas_call(
        _ada_body,
        grid=(n // bn,),
        in_specs=[
            pl.BlockSpec((bsz, d), lambda j: (0, 0)),
            pl.BlockSpec((d, bn), lambda j: (0, j)),
            pl.BlockSpec((1, bn), lambda j: (0, j)),
        ],
        out_specs=pl.BlockSpec((bsz, bn), lambda j: (0, j)),
        out_shape=jax.ShapeDtypeStruct((bsz, n), jnp.float32),
        name="ada_mod",
    )(c, w_ada, b_ada.reshape(1, n))


def _merged(vector_pieces, matrix_pieces):
    total_v = sum(c for c, _ in vector_pieces)
    total_m = sum(c for c, _ in matrix_pieces)
    order = []
    iv = im = 0
    done_v = done_m = 0.0
    while iv < len(vector_pieces) or im < len(matrix_pieces):
        take_m = iv == len(vector_pieces) or (
            im < len(matrix_pieces) and done_m / total_m <= done_v / total_v)
        if take_m:
            cost, fn = matrix_pieces[im]
            im += 1
            done_m += cost
        else:
            cost, fn = vector_pieces[iv]
            iv += 1
            done_v += cost
        order.append(fn)
    return order


def _layer_body(run_ref, xa_ref, xf_ref, shift_ref, scale_ref, gate_ref, ng_ref,
                win_hbm, cw_ref, cb_ref, clg_ref, clb_ref, slg_ref, slb_ref,
                ws_ref, bs_ref, wout_hbm, fg_ref, o_ref,
                win_s, wout_s, ws_s, stage_sem,
                h0, h1, z0, z1, y0, y1, o0, o1, g_s, co_s, sg_s, yb_s,
                *, chunks_per_seq, n_chunks, apply_final_norm):
    d = xa_ref.shape[2]
    n_cb = d // LANES
    n_heads = d // HEAD_DIM
    h_ring, z_ring, y_ring, o_ring = (h0, h1), (z0, z1), (y0, y1), (o0, o1)
    step = pl.program_id(0)

    @pl.when(step == 0)
    def _():
        in_stage, out_stage = (z0, z1), (o0, o1)
        in_rows, out_rows = z0.shape[0], o0.shape[0]
        blocks = [(win_hbm, win_s, r0, in_rows, in_stage[k % 2], k % 2)
                  for k, r0 in enumerate(range(0, win_s.shape[0], in_rows))]
        blocks += [(wout_hbm, wout_s, r0, out_rows, out_stage[k % 2], 2 + k % 2)
                   for k, r0 in enumerate(range(0, wout_s.shape[0], out_rows))]
        depth = 2

        def fetch(idx):
            src, _, r0, rows, buf, sem = blocks[idx]
            return pltpu.make_async_copy(src.at[pl.ds(r0, rows), :], buf, stage_sem.at[sem])

        n_in = win_s.shape[0] // in_rows
        for first in (0, n_in):
            for k in range(depth):
                fetch(first + k).start()
        for hd in range(n_heads):
            ws_s[hd * CHUNK:(hd + 1) * CHUNK, :] = ws_ref[hd].astype(jnp.bfloat16)
        for idx, (_, dst, r0, rows, buf, _) in enumerate(blocks):
            fetch(idx).wait()
            dst[r0:r0 + rows, :] = buf[...].astype(jnp.bfloat16)
            last = n_in if idx < n_in else len(blocks)
            if idx + depth < last:
                fetch(idx + depth).start()
        for ref in (z0, z1, y0, y1, o0, o1, g_s, sg_s, yb_s):
            ref[...] = jnp.zeros(ref.shape, ref.dtype)

    def batch_of(k):
        return lax.div(jnp.clip(k, 0, n_chunks - 1), chunks_per_seq)

    def seq_pos(k):
        return lax.rem(k + chunks_per_seq, chunks_per_seq)

    def iteration(j, par):
        p2 = par % 2
        alt = 1 - p2
        x_rows = slice(par * CHUNK, (par + 1) * CHUNK)
        cur2 = slice(p2 * CHUNK, (p2 + 1) * CHUNK)
        alt2 = slice(alt * CHUNK, (alt + 1) * CHUNK)

        def g_base(offset):
            return ((par + offset) % G_SLOTS) * (SLOT_POS * n_cb)

        vec, mat = [], []

        def stage_a():
            b_a = batch_of(j)
            xv = xa_ref[0, x_rows, :]
            ms = jnp.mean(xv * xv, axis=-1, keepdims=True)
            mult = ng_ref[...] * (1.0 + scale_ref[b_a])
            h = xv * lax.rsqrt(ms + EPS) * mult + shift_ref[b_a]
            h_ring[p2][...] = h.astype(jnp.bfloat16)

        def dot_piece(dst, lhs_ref, w_s, c0):
            def run():
                dst[:, c0:c0 + MXU_COLS] = jnp.dot(lhs_ref[...], w_s[:, c0:c0 + MXU_COLS],
                                                   preferred_element_type=jnp.float32)
            return run
        for c0 in range(0, d, MXU_COLS):
            mat.append((2 * d, dot_piece(o_ring[alt], y_ring[alt], wout_s, c0)))
        for c0 in range(0, 6 * d, MXU_COLS):
            mat.append((d, dot_piece(z_ring[p2], h_ring[p2], win_s, c0)))

        def stage_f():
            b_f = batch_of(j - 4)
            xo = xf_ref[0, x_rows, :] + gate_ref[b_f] * o_ring[p2][...]
            if apply_final_norm:
                ms = jnp.mean(xo * xo, axis=-1, keepdims=True)
                xo = xo * lax.rsqrt(ms + EPS) * fg_ref[...]
            o_ref[0, x_rows, :] = xo
        vec.append((200, stage_f))

        kc = j - 1
        zc = z_ring[alt]

        def glu_piece(s):
            def run():
                main = g_base(-1) + HALO * n_cb
                head_next = g_base(0)
                tail_prev = g_base(-2) + (HALO + CHUNK) * n_cb
                keep_head = seq_pos(kc) != chunks_per_seq - 1
                keep_tail = seq_pos(kc) != 0
                g = zc[:, s * LANES:(s + 1) * LANES] * _sigmoid(zc[:, d + s * LANES:d + (s + 1) * LANES])
                g_s[pl.ds(main + s, CHUNK, stride=n_cb), :] = g
                g_s[pl.ds(head_next + s, HALO, stride=n_cb), :] = jnp.where(
                    keep_head, g[CHUNK - HALO:CHUNK], 0.0)
                g_s[pl.ds(tail_prev + s, HALO, stride=n_cb), :] = jnp.where(
                    keep_tail, g[0:HALO], 0.0)
            return run
        for s in range(n_cb):
            vec.append((30, glu_piece(s)))

        def stage_c_gate():
            sg_s[alt2, :] = _silu(zc[:, 2 * d:3 * d])
        vec.append((160, stage_c_gate))

        def stage_c_gmlp():
            vn = _layer_norm(zc[:, 4 * d:5 * d], slg_ref[...], slb_ref[...]).astype(jnp.bfloat16)
            mixed = jnp.concatenate(
                [jnp.dot(ws_s[hd * CHUNK:(hd + 1) * CHUNK, :],
                         vn[:, hd * HEAD_DIM:(hd + 1) * HEAD_DIM],
                         preferred_element_type=jnp.float32) for hd in range(n_heads)],
                axis=-1) + bs_ref[...]
            yb = zc[:, 3 * d:4 * d] * mixed * _silu(zc[:, 5 * d:6 * d])
            yb_s[alt2, :] = yb.astype(jnp.bfloat16)
        vec.append((500, stage_c_gmlp))

        p = CONV_POS_BLOCK

        def conv_piece(blk):
            def run():
                src = g_base(-2) + (HALO - CONV_HALF) * n_cb
                acc = [cb_ref[...]] * p
                for tap0 in range(0, CONV_WIDTH, CONV_TAP_GROUP):
                    n_taps = min(CONV_TAP_GROUP, CONV_WIDTH - tap0)
                    first = src + (blk * p + tap0) * n_cb
                    window = [g_s[first + e * n_cb:first + (e + 1) * n_cb, :]
                              for e in range(p + n_taps - 1)]
                    for t in range(n_taps):
                        w_tap = cw_ref[tap0 + t]
                        acc = [acc[i] + window[i + t] * w_tap for i in range(p)]
                for i in range(p):
                    co_s[(blk * p + i) * n_cb:(blk * p + i + 1) * n_cb, :] = acc[i]
            return run
        for blk in range(CHUNK // p):
            vec.append((125, conv_piece(blk)))

        def stage_d_tail():
            cv = jnp.concatenate(
                [co_s[pl.ds(s, CHUNK, stride=n_cb), :] for s in range(n_cb)], axis=-1)
            ya = _silu(_layer_norm(cv, clg_ref[...], clb_ref[...])) * sg_s[cur2, :]
            y_ring[p2][:, 0:d] = ya.astype(jnp.bfloat16)
            y_ring[p2][:, d:2 * d] = yb_s[cur2, :]
        vec.append((500, stage_d_tail))

        stage_a()
        for fn in _merged(vec, mat):
            fn()

    for par in range(CHUNKS_PER_STEP):
        pl.when(run_ref[par] == 1)(
            functools.partial(iteration, step * CHUNKS_PER_STEP + par, par))


def _layer_call(x, shift, scale, gate, norm_g, w_in, conv_w, conv_b, conv_ln_g, conv_ln_b,
                sg_ln_g, sg_ln_b, w_s, b_s, w_out, final_g, *, apply_final_norm):
    bsz, seq, d = x.shape
    assert seq % (CHUNK * CHUNKS_PER_STEP) == 0 and PIPE_LAG % CHUNKS_PER_STEP == 0
    assert CHUNKS_PER_STEP % G_SLOTS == 0 and CHUNKS_PER_STEP % 2 == 0
    assert d % LANES == 0 and d // LANES == SUBLANES and w_in.shape == (d, 6 * d)
    assert d % MXU_COLS == 0 and d % CHUNK == 0
    n_heads = w_s.shape[0]
    assert w_s.shape == (n_heads, CHUNK, CHUNK) and n_heads * HEAD_DIM == d
    n_cb = d // LANES
    tile = CHUNK * CHUNKS_PER_STEP
    n_tiles = bsz * seq // tile
    lag_steps = PIPE_LAG // CHUNKS_PER_STEP
    n_steps = n_tiles + lag_steps

    const2 = lambda i, run: (0, 0)
    const3 = lambda i, run: (0, 0, 0)
    load_tile = lambda i, run: (jnp.minimum(i, n_tiles - 1), 0, 0)
    store_tile = lambda i, run: (jnp.maximum(i - lag_steps, 0), 0, 0)
    once = pl.Buffered(1)
    row = lambda v: v.reshape(1, d).astype(jnp.float32)

    bs_full = jnp.repeat(jnp.transpose(b_s), HEAD_DIM, axis=1).astype(jnp.float32)

    in_specs = [
        pl.BlockSpec((1, tile, d), load_tile),
        pl.BlockSpec((1, tile, d), store_tile),
        pl.BlockSpec((bsz, 1, d), const3, pipeline_mode=once),
        pl.BlockSpec((bsz, 1, d), const3, pipeline_mode=once),
        pl.BlockSpec((bsz, 1, d), const3, pipeline_mode=once),
        pl.BlockSpec((1, d), const2, pipeline_mode=once),
        pl.BlockSpec(memory_space=pl.ANY),
        pl.BlockSpec((CONV_WIDTH, n_cb, LANES), const3, pipeline_mode=once),
        pl.BlockSpec((n_cb, LANES), const2, pipeline_mode=once),
        pl.BlockSpec((1, d), const2, pipeline_mode=once),
        pl.BlockSpec((1, d), const2, pipeline_mode=once),
        pl.BlockSpec((1, d), const2, pipeline_mode=once),
        pl.BlockSpec((1, d), const2, pipeline_mode=once),
        pl.BlockSpec((n_heads, CHUNK, CHUNK), const3, pipeline_mode=once),
        pl.BlockSpec((CHUNK, d), const2, pipeline_mode=once),
        pl.BlockSpec(memory_space=pl.ANY),
        pl.BlockSpec((1, d), const2, pipeline_mode=once),
    ]
    scratch_shapes = [
        pltpu.VMEM((d, 6 * d), jnp.bfloat16),
        pltpu.VMEM((2 * d, d), jnp.bfloat16),
        pltpu.VMEM((n_heads * CHUNK, CHUNK), jnp.bfloat16),
        pltpu.SemaphoreType.DMA((4,)),
        pltpu.VMEM((CHUNK, d), jnp.bfloat16),
        pltpu.VMEM((CHUNK, d), jnp.bfloat16),
        pltpu.VMEM((CHUNK, 6 * d), jnp.float32),
        pltpu.VMEM((CHUNK, 6 * d), jnp.float32),
        pltpu.VMEM((CHUNK, 2 * d), jnp.bfloat16),
        pltpu.VMEM((CHUNK, 2 * d), jnp.bfloat16),
        pltpu.VMEM((CHUNK, d), jnp.float32),
        pltpu.VMEM((CHUNK, d), jnp.float32),
        pltpu.VMEM((G_SLOTS * SLOT_POS * n_cb, LANES), jnp.float32),
        pltpu.VMEM((CHUNK * n_cb, LANES), jnp.float32),
        pltpu.VMEM((2 * CHUNK, d), jnp.float32),
        pltpu.VMEM((2 * CHUNK, d), jnp.bfloat16),
    ]
    out = pl.pallas_call(
        functools.partial(_layer_body, chunks_per_seq=seq // CHUNK,
                          n_chunks=bsz * seq // CHUNK, apply_final_norm=apply_final_norm),
        grid_spec=pltpu.PrefetchScalarGridSpec(
            num_scalar_prefetch=1,
            grid=(n_steps,),
            in_specs=in_specs,
            out_specs=pl.BlockSpec((1, tile, d), store_tile),
            scratch_shapes=scratch_shapes),
        out_shape=jax.ShapeDtypeStruct((n_tiles, tile, d), jnp.float32),
        compiler_params=pltpu.CompilerParams(
            dimension_semantics=("arbitrary",),
            vmem_limit_bytes=VMEM_LIMIT_BYTES),
        name="hybrid_layer",
    )(jnp.ones((CHUNKS_PER_STEP,), jnp.int32),
      x.reshape(n_tiles, tile, d), x.reshape(n_tiles, tile, d), shift, scale, gate,
      row(norm_g), w_in.astype(jnp.float32),
      conv_w.reshape(CONV_WIDTH, n_cb, LANES), conv_b.reshape(n_cb, LANES),
      row(conv_ln_g), row(conv_ln_b), row(sg_ln_g), row(sg_ln_b),
      w_s.astype(jnp.float32), bs_full, w_out.astype(jnp.float32), row(final_g))
    return out.reshape(bsz, seq, d)


def kernel(x, c, w_ada, b_ada, norm_g, w_in, conv_w, conv_b, conv_ln_g, conv_ln_b, sg_ln_g, sg_ln_b, w_s, b_s, w_out, final_g):
    depth = w_in.shape[0]
    bsz, _, d = x.shape
    for l in range(depth):
        mod = _ada_call(c, w_ada[l], b_ada[l])
        shift, scale, gate = (mod[:, k * d:(k + 1) * d].reshape(bsz, 1, d) for k in range(3))
        x = _layer_call(x, shift, scale, gate, norm_g[l], w_in[l], conv_w[l], conv_b[l],
                        conv_ln_g[l], conv_ln_b[l], sg_ln_g[l], sg_ln_b[l], w_s[l], b_s[l],
                        w_out[l], final_g, apply_final_norm=(l == depth - 1))
    return x
```

```python
import functools

import jax
import jax.numpy as jnp
from jax import lax
from jax.experimental import pallas as pl
from jax.experimental.pallas import tpu as pltpu

EPS = 1e-6
CONV_WIDTH = 31
CONV_HALF = CONV_WIDTH // 2
CHUNK = 128
HEAD_DIM = 128
LANES = 128
SUBLANES = 8
HALO = 16
SLOT_POS = CHUNK + 2 * HALO
CHUNKS_PER_STEP = 4
G_SLOTS = 4
PIPE_LAG = 4
CONV_PAIR_BLOCK = 4
CONV_TAP_GROUP = 8
MXU_COLS = 512
W_STREAM_DEPTH = 4
VMEM_LIMIT_BYTES = 56 * 1024 * 1024


def _sigmoid(v):
    return 1.0 / (1.0 + jnp.exp(-v))


def _silu(v):
    return v * _sigmoid(v)


def _layer_norm(v, g, b):
    mu = jnp.mean(v, axis=-1, keepdims=True)
    d = v - mu
    var = jnp.mean(d * d, axis=-1, keepdims=True)
    return d * lax.rsqrt(var + EPS) * g + b


def _ada_body(c_ref, w_ref, b_ref, o_ref):
    c_act = _silu(c_ref[...])
    o_ref[...] = jnp.dot(c_act, w_ref[...], preferred_element_type=jnp.float32,
                         precision=lax.Precision.HIGHEST) + b_ref[...]


def _ada_call(c, w_ada, b_ada):
    bsz, d = c.shape
    n = w_ada.shape[1]
    bn = 512
    return pl.pallas_call(
        _ada_body,
        grid=(n // bn,),
        in_specs=[
            pl.BlockSpec((bsz, d), lambda j: (0, 0)),
            pl.BlockSpec((d, bn), lambda j: (0, j)),
            pl.BlockSpec((1, bn), lambda j: (0, j)),
        ],
        out_specs=pl.BlockSpec((bsz, bn), lambda j: (0, j)),
        out_shape=jax.ShapeDtypeStruct((bsz, n), jnp.float32),
        name="ada_mod",
    )(c, w_ada, b_ada.reshape(1, n))


def _merged(vector_pieces, matrix_pieces):
    total_v = sum(c for c, _ in vector_pieces)
    total_m = sum(c for c, _ in matrix_pieces)
    order = []
    iv = im = 0
    done_v = done_m = 0.0
    while iv < len(vector_pieces) or im < len(matrix_pieces):
        take_m = iv == len(vector_pieces) or (
            im < len(matrix_pieces) and done_m / total_m <= done_v / total_v)
        if take_m:
            cost, fn = matrix_pieces[im]
            im += 1
            done_m += cost
        else:
            cost, fn = vector_pieces[iv]
            iv += 1
            done_v += cost
        order.append(fn)
    return order


def _layer_body(run_ref, xa_ref, xf_ref, shift_ref, scale_ref, gate_ref, ng_ref,
                win_hbm, cwp_ref, cbp_ref, clg_ref, clb_ref, slg_ref, slb_ref,
                ws_ref, bs_ref, wout_hbm, fg_ref, o_ref,
                win_s, wout_s, ws_s, stage_sem,
                h0, h1, z0, z1, y0, y1, o0, o1, g_s, ga_s, gb_s, co_s, sg_s, yb_s,
                *, chunks_per_seq, n_chunks, apply_final_norm):
    d = xa_ref.shape[2]
    n_cb = d // LANES
    n_heads = d // HEAD_DIM
    h_ring, z_ring, y_ring, o_ring = (h0, h1), (z0, z1), (y0, y1), (o0, o1)
    step = pl.program_id(0)

    @pl.when(step == 0)
    def _():
        depth = W_STREAM_DEPTH
        in_rows, out_rows = z0.shape[0] // 2, o0.shape[0]
        in_stage = (z0.at[0:in_rows], z0.at[in_rows:2 * in_rows],
                    z1.at[0:in_rows], z1.at[in_rows:2 * in_rows])
        out_stage = (o0, o1, sg_s.at[0:out_rows], sg_s.at[out_rows:2 * out_rows])
        blocks = [(win_hbm, win_s, r0, in_rows, in_stage[k % depth], k % depth)
                  for k, r0 in enumerate(range(0, win_s.shape[0], in_rows))]
        blocks += [(wout_hbm, wout_s, r0, out_rows, out_stage[k % depth], depth + k % depth)
                   for k, r0 in enumerate(range(0, wout_s.shape[0], out_rows))]

        def fetch(idx):
            src, _, r0, rows, buf, sem = blocks[idx]
            return pltpu.make_async_copy(src.at[pl.ds(r0, rows), :], buf, stage_sem.at[sem])

        n_in = win_s.shape[0] // in_rows
        for first in (0, n_in):
            for k in range(depth):
                fetch(first + k).start()
        for hd in range(n_heads):
            ws_s[hd * CHUNK:(hd + 1) * CHUNK, :] = ws_ref[hd].astype(jnp.bfloat16)
        for idx, (_, dst, r0, rows, buf, _) in enumerate(blocks):
            fetch(idx).wait()
            dst[r0:r0 + rows, :] = buf[...].astype(jnp.bfloat16)
            last = n_in if idx < n_in else len(blocks)
            if idx + depth < last:
                fetch(idx + depth).start()
        for ref in (z0, z1, y0, y1, o0, o1, g_s, gb_s, sg_s, yb_s):
            ref[...] = jnp.zeros(ref.shape, ref.dtype)

    def batch_of(k):
        return lax.div(jnp.clip(k, 0, n_chunks - 1), chunks_per_seq)

    def seq_pos(k):
        return lax.rem(k + chunks_per_seq, chunks_per_seq)

    def iteration(j, par):
        p2 = par % 2
        alt = 1 - p2
        x_rows = slice(par * CHUNK, (par + 1) * CHUNK)
        cur2 = slice(p2 * CHUNK, (p2 + 1) * CHUNK)
        alt2 = slice(alt * CHUNK, (alt + 1) * CHUNK)

        def g_base(offset):
            return ((par + offset) % G_SLOTS) * (SLOT_POS * n_cb)

        vec, mat = [], []

        def stage_a():
            b_a = batch_of(j)
            xv = xa_ref[0, x_rows, :]
            ms = jnp.mean(xv * xv, axis=-1, keepdims=True)
            mult = ng_ref[...] * (1.0 + scale_ref[b_a])
            h = xv * lax.rsqrt(ms + EPS) * mult + shift_ref[b_a]
            h_ring[p2][...] = h.astype(jnp.bfloat16)

        def dot_piece(dst, lhs_ref, w_s, c0):
            def run():
                dst[:, c0:c0 + MXU_COLS] = jnp.dot(lhs_ref[...], w_s[:, c0:c0 + MXU_COLS],
                                                   preferred_element_type=jnp.float32)
            return run
        for c0 in range(0, d, MXU_COLS):
            mat.append((2 * d, dot_piece(o_ring[alt], y_ring[alt], wout_s, c0)))
        for c0 in range(0, 6 * d, MXU_COLS):
            mat.append((d, dot_piece(z_ring[p2], h_ring[p2], win_s, c0)))

        def stage_f():
            b_f = batch_of(j - 4)
            xo = xf_ref[0, x_rows, :] + gate_ref[b_f] * o_ring[p2][...]
            if apply_final_norm:
                ms = jnp.mean(xo * xo, axis=-1, keepdims=True)
                xo = xo * lax.rsqrt(ms + EPS) * fg_ref[...]
            o_ref[0, x_rows, :] = xo
        vec.append((200, stage_f))

        kc = j - 1
        zc = z_ring[alt]

        def glu_piece(s):
            def run():
                main = g_base(-1) + HALO * n_cb
                head_next = g_base(0)
                tail_prev = g_base(-2) + (HALO + CHUNK) * n_cb
                keep_head = seq_pos(kc) != chunks_per_seq - 1
                keep_tail = seq_pos(kc) != 0
                g = zc[:, s * LANES:(s + 1) * LANES] * _sigmoid(zc[:, d + s * LANES:d + (s + 1) * LANES])
                g_s[pl.ds(main + s, CHUNK, stride=n_cb), :] = g
                g_s[pl.ds(head_next + s, HALO, stride=n_cb), :] = jnp.where(
                    keep_head, g[CHUNK - HALO:CHUNK], 0.0)
                g_s[pl.ds(tail_prev + s, HALO, stride=n_cb), :] = jnp.where(
                    keep_tail, g[0:HALO], 0.0)
            return run
        for s in range(n_cb):
            vec.append((30, glu_piece(s)))

        def stage_c_gate():
            sg_s[alt2, :] = _silu(zc[:, 2 * d:3 * d])
        vec.append((160, stage_c_gate))

        def stage_c_gmlp():
            vn = _layer_norm(zc[:, 4 * d:5 * d], slg_ref[...], slb_ref[...]).astype(jnp.bfloat16)
            mixed = jnp.concatenate(
                [jnp.dot(ws_s[hd * CHUNK:(hd + 1) * CHUNK, :],
                         vn[:, hd * HEAD_DIM:(hd + 1) * HEAD_DIM],
                         preferred_element_type=jnp.float32) for hd in range(n_heads)],
                axis=-1) + bs_ref[...]
            yb = zc[:, 3 * d:4 * d] * mixed * _silu(zc[:, 5 * d:6 * d])
            yb_s[alt2, :] = yb.astype(jnp.bfloat16)
        vec.append((500, stage_c_gmlp))

        pair_rows = 2 * n_cb
        n_pairs = SLOT_POS // 2

        def pack_piece():
            base = g_base(-2)
            ga_s[...] = g_s[base:base + n_pairs * pair_rows, :].astype(jnp.bfloat16)
            gb_s[0:(n_pairs - 1) * pair_rows, :] = g_s[
                base + n_cb:base + n_cb + (n_pairs - 1) * pair_rows, :].astype(jnp.bfloat16)
        vec.append((40, pack_piece))

        def tap_source(tap):
            first = HALO - CONV_HALF + tap
            return (ga_s, first // 2) if first % 2 == 0 else (gb_s, first // 2)

        def conv_piece(blk):
            def run():
                u0 = blk * CONV_PAIR_BLOCK
                acc = [cbp_ref[...]] * CONV_PAIR_BLOCK
                for tap0 in range(0, CONV_WIDTH, CONV_TAP_GROUP):
                    taps = range(tap0, min(tap0 + CONV_TAP_GROUP, CONV_WIDTH))
                    loaded = {}

                    def src(tap, q):
                        ref, off = tap_source(tap)
                        key = (id(ref), u0 + q + off)
                        if key not in loaded:
                            r0 = (u0 + q + off) * pair_rows
                            loaded[key] = ref[r0:r0 + pair_rows, :]
                        return loaded[key]

                    weights = {tap: cwp_ref[tap] for tap in taps}
                    for q in range(CONV_PAIR_BLOCK):
                        terms = [src(tap, q) * weights[tap] for tap in taps]
                        while len(terms) > 1:
                            terms = [terms[i] + terms[i + 1] if i + 1 < len(terms) else terms[i]
                                     for i in range(0, len(terms), 2)]
                        acc[q] = acc[q] + terms[0].astype(jnp.float32)
                for q in range(CONV_PAIR_BLOCK):
                    r0 = (u0 + q) * pair_rows
                    co_s[r0:r0 + pair_rows, :] = acc[q]
            return run
        for blk in range(CHUNK // 2 // CONV_PAIR_BLOCK):
            vec.append((80, conv_piece(blk)))

        def stage_d_tail():
            cv = jnp.concatenate(
                [co_s[pl.ds(s, CHUNK, stride=n_cb), :] for s in range(n_cb)], axis=-1)
            ya = _silu(_layer_norm(cv, clg_ref[...], clb_ref[...])) * sg_s[cur2, :]
            y_ring[p2][:, 0:d] = ya.astype(jnp.bfloat16)
            y_ring[p2][:, d:2 * d] = yb_s[cur2, :]
        vec.append((500, stage_d_tail))

        stage_a()
        for fn in _merged(vec, mat):
            fn()

    for par in range(CHUNKS_PER_STEP):
        pl.when(run_ref[par] == 1)(
            functools.partial(iteration, step * CHUNKS_PER_STEP + par, par))


def _layer_call(x, shift, scale, gate, norm_g, w_in, conv_w, conv_b, conv_ln_g, conv_ln_b,
                sg_ln_g, sg_ln_b, w_s, b_s, w_out, final_g, *, apply_final_norm):
    bsz, seq, d = x.shape
    assert seq % (CHUNK * CHUNKS_PER_STEP) == 0 and PIPE_LAG % CHUNKS_PER_STEP == 0
    assert CHUNKS_PER_STEP % G_SLOTS == 0 and CHUNKS_PER_STEP % 2 == 0
    assert d % LANES == 0 and d // LANES == SUBLANES and w_in.shape == (d, 6 * d)
    assert d % MXU_COLS == 0 and d % CHUNK == 0
    n_heads = w_s.shape[0]
    assert w_s.shape == (n_heads, CHUNK, CHUNK) and n_heads * HEAD_DIM == d
    n_cb = d // LANES
    tile = CHUNK * CHUNKS_PER_STEP
    n_tiles = bsz * seq // tile
    lag_steps = PIPE_LAG // CHUNKS_PER_STEP
    n_steps = n_tiles + lag_steps

    const2 = lambda i, run: (0, 0)
    const3 = lambda i, run: (0, 0, 0)
    load_tile = lambda i, run: (jnp.minimum(i, n_tiles - 1), 0, 0)
    store_tile = lambda i, run: (jnp.maximum(i - lag_steps, 0), 0, 0)
    once = pl.Buffered(1)
    row = lambda v: v.reshape(1, d).astype(jnp.float32)

    cw_tiles = conv_w.reshape(CONV_WIDTH, n_cb, LANES)
    cw_pairs = jnp.concatenate([cw_tiles, cw_tiles], axis=1).astype(jnp.bfloat16)
    cb_tile = conv_b.reshape(n_cb, LANES).astype(jnp.float32)
    cb_pairs = jnp.concatenate([cb_tile, cb_tile], axis=0)
    bs_full = jnp.repeat(jnp.transpose(b_s), HEAD_DIM, axis=1).astype(jnp.float32)

    in_specs = [
        pl.BlockSpec((1, tile, d), load_tile),
        pl.BlockSpec((1, tile, d), store_tile),
        pl.BlockSpec((bsz, 1, d), const3, pipeline_mode=once),
        pl.BlockSpec((bsz, 1, d), const3, pipeline_mode=once),
        pl.BlockSpec((bsz, 1, d), const3, pipeline_mode=once),
        pl.BlockSpec((1, d), const2, pipeline_mode=once),
        pl.BlockSpec(memory_space=pl.ANY),
        pl.BlockSpec((CONV_WIDTH, 2 * n_cb, LANES), const3, pipeline_mode=once),
        pl.BlockSpec((2 * n_cb, LANES), const2, pipeline_mode=once),
        pl.BlockSpec((1, d), const2, pipeline_mode=once),
        pl.BlockSpec((1, d), const2, pipeline_mode=once),
        pl.BlockSpec((1, d), const2, pipeline_mode=once),
        pl.BlockSpec((1, d), const2, pipeline_mode=once),
        pl.BlockSpec((n_heads, CHUNK, CHUNK), const3, pipeline_mode=once),
        pl.BlockSpec((CHUNK, d), const2, pipeline_mode=once),
        pl.BlockSpec(memory_space=pl.ANY),
        pl.BlockSpec((1, d), const2, pipeline_mode=once),
    ]
    scratch_shapes = [
        pltpu.VMEM((d, 6 * d), jnp.bfloat16),
        pltpu.VMEM((2 * d, d), jnp.bfloat16),
        pltpu.VMEM((n_heads * CHUNK, CHUNK), jnp.bfloat16),
        pltpu.SemaphoreType.DMA((2 * W_STREAM_DEPTH,)),
        pltpu.VMEM((CHUNK, d), jnp.bfloat16),
        pltpu.VMEM((CHUNK, d), jnp.bfloat16),
        pltpu.VMEM((CHUNK, 6 * d), jnp.float32),
        pltpu.VMEM((CHUNK, 6 * d), jnp.float32),
        pltpu.VMEM((CHUNK, 2 * d), jnp.bfloat16),
        pltpu.VMEM((CHUNK, 2 * d), jnp.bfloat16),
        pltpu.VMEM((CHUNK, d), jnp.float32),
        pltpu.VMEM((CHUNK, d), jnp.float32),
        pltpu.VMEM((G_SLOTS * SLOT_POS * n_cb, LANES), jnp.float32),
        pltpu.VMEM((SLOT_POS * n_cb, LANES), jnp.bfloat16),
        pltpu.VMEM((SLOT_POS * n_cb, LANES), jnp.bfloat16),
        pltpu.VMEM((CHUNK * n_cb, LANES), jnp.float32),
        pltpu.VMEM((2 * CHUNK, d), jnp.float32),
        pltpu.VMEM((2 * CHUNK, d), jnp.bfloat16),
    ]
    out = pl.pallas_call(
        functools.partial(_layer_body, chunks_per_seq=seq // CHUNK,
                          n_chunks=bsz * seq // CHUNK, apply_final_norm=apply_final_norm),
        grid_spec=pltpu.PrefetchScalarGridSpec(
            num_scalar_prefetch=1,
            grid=(n_steps,),
            in_specs=in_specs,
            out_specs=pl.BlockSpec((1, tile, d), store_tile),
            scratch_shapes=scratch_shapes),
        out_shape=jax.ShapeDtypeStruct((n_tiles, tile, d), jnp.float32),
        compiler_params=pltpu.CompilerParams(
            dimension_semantics=("arbitrary",),
            vmem_limit_bytes=VMEM_LIMIT_BYTES),
        name="hybrid_layer",
    )(jnp.ones((CHUNKS_PER_STEP,), jnp.int32),
      x.reshape(n_tiles, tile, d), x.reshape(n_tiles, tile, d), shift, scale, gate,
      row(norm_g), w_in.astype(jnp.float32),
      cw_pairs, cb_pairs,
      row(conv_ln_g), row(conv_ln_b), row(sg_ln_g), row(sg_ln_b),
      w_s.astype(jnp.float32), bs_full, w_out.astype(jnp.float32), row(final_g))
    return out.reshape(bsz, seq, d)


def kernel(x, c, w_ada, b_ada, norm_g, w_in, conv_w, conv_b, conv_ln_g, conv_ln_b, sg_ln_g, sg_ln_b, w_s, b_s, w_out, final_g):
    depth = w_in.shape[0]
    bsz, _, d = x.shape
    for l in range(depth):
        mod = _ada_call(c, w_ada[l], b_ada[l])
        shift, scale, gate = (mod[:, k * d:(k + 1) * d].reshape(bsz, 1, d) for k in range(3))
        x = _layer_call(x, shift, scale, gate, norm_g[l], w_in[l], conv_w[l], conv_b[l],
                        conv_ln_g[l], conv_ln_b[l], sg_ln_g[l], sg_ln_b[l], w_s[l], b_s[l],
                        w_out[l], final_g, apply_final_norm=(l == depth - 1))
    return x
```

```python
import functools

import jax
import jax.numpy as jnp
from jax import lax
from jax.experimental import pallas as pl
from jax.experimental.pallas import tpu as pltpu

EPS = 1e-6
CONV_WIDTH = 31
CONV_HALF = CONV_WIDTH // 2
CHUNK = 128
HEAD_DIM = 128
LANES = 128
SUBLANES = 8
HALO = 16
SLOT_POS = CHUNK + 2 * HALO
CHUNKS_PER_STEP = 4
G_SLOTS = 4
PIPE_LAG = 4
CONV_POS_BLOCK = 8
CONV_TAP_GROUP = 8
MXU_COLS = 512
W_IN_STREAM_DEPTH = 8
W_OUT_STREAM_DEPTH = 4
VMEM_LIMIT_BYTES = 56 * 1024 * 1024


def _sigmoid(v):
    return 1.0 / (1.0 + jnp.exp(-v))


def _silu(v):
    return v * _sigmoid(v)


def _layer_norm(v, g, b):
    mu = jnp.mean(v, axis=-1, keepdims=True)
    d = v - mu
    var = jnp.mean(d * d, axis=-1, keepdims=True)
    return d * lax.rsqrt(var + EPS) * g + b


def _ada_body(c_ref, w_ref, b_ref, o_ref):
    c_act = _silu(c_ref[...])
    o_ref[...] = jnp.dot(c_act, w_ref[...], preferred_element_type=jnp.float32,
                         precision=lax.Precision.HIGHEST) + b_ref[...]


def _ada_call(c, w_ada, b_ada):
    bsz, d = c.shape
    n = w_ada.shape[1]
    bn = 512
    return pl.pallas_call(
        _ada_body,
        grid=(n // bn,),
        in_specs=[
            pl.BlockSpec((bsz, d), lambda j: (0, 0)),
            pl.BlockSpec((d, bn), lambda j: (0, j)),
            pl.BlockSpec((1, bn), lambda j: (0, j)),
        ],
        out_specs=pl.BlockSpec((bsz, bn), lambda j: (0, j)),
        out_shape=jax.ShapeDtypeStruct((bsz, n), jnp.float32),
        name="ada_mod",
    )(c, w_ada, b_ada.reshape(1, n))


def _merged(vector_pieces, matrix_pieces):
    total_v = sum(c for c, _ in vector_pieces)
    total_m = sum(c for c, _ in matrix_pieces)
    order = []
    iv = im = 0
    done_v = done_m = 0.0
    while iv < len(vector_pieces) or im < len(matrix_pieces):
        take_m = iv == len(vector_pieces) or (
            im < len(matrix_pieces) and done_m / total_m <= done_v / total_v)
        if take_m:
            cost, fn = matrix_pieces[im]
            im += 1
            done_m += cost
        else:
            cost, fn = vector_pieces[iv]
            iv += 1
            done_v += cost
        order.append(fn)
    return order


def _layer_body(run_ref, xa_ref, xf_ref, shift_ref, scale_ref, gate_ref, ng_ref,
                win_hbm, cw_ref, cb_ref, clg_ref, clb_ref, slg_ref, slb_ref,
                ws_ref, bs_ref, wout_hbm, fg_ref, o_ref,
                win_s, wout_s, ws_s, stage_sem,
                h0, h1, z0, z1, y0, y1, o0, o1, g_s, co_s, sg_s, yb_s,
                *, chunks_per_seq, n_chunks, apply_final_norm):
    d = xa_ref.shape[2]
    n_cb = d // LANES
    n_heads = d // HEAD_DIM
    h_ring, z_ring, y_ring, o_ring = (h0, h1), (z0, z1), (y0, y1), (o0, o1)
    step = pl.program_id(0)

    @pl.when(step == 0)
    def _():
        in_depth, out_depth = W_IN_STREAM_DEPTH, W_OUT_STREAM_DEPTH
        in_rows, out_rows = 2 * z0.shape[0] // in_depth, o0.shape[0]
        in_stage = [z.at[r0:r0 + in_rows] for z in (z0, z1) for r0 in range(0, z.shape[0], in_rows)]
        out_stage = (o0, o1, sg_s.at[0:out_rows], sg_s.at[out_rows:2 * out_rows])
        blocks = [(win_hbm, win_s, r0, in_rows, in_stage[k % in_depth], k % in_depth)
                  for k, r0 in enumerate(range(0, win_s.shape[0], in_rows))]
        blocks += [(wout_hbm, wout_s, r0, out_rows, out_stage[k % out_depth], in_depth + k % out_depth)
                   for k, r0 in enumerate(range(0, wout_s.shape[0], out_rows))]

        def fetch(idx):
            src, _, r0, rows, buf, sem = blocks[idx]
            return pltpu.make_async_copy(src.at[pl.ds(r0, rows), :], buf, stage_sem.at[sem])

        n_in = win_s.shape[0] // in_rows
        for first, depth in ((0, in_depth), (n_in, out_depth)):
            for k in range(depth):
                fetch(first + k).start()
        for hd in range(n_heads):
            ws_s[hd * CHUNK:(hd + 1) * CHUNK, :] = ws_ref[hd].astype(jnp.bfloat16)
        for idx, (_, dst, r0, rows, buf, _) in enumerate(blocks):
            fetch(idx).wait()
            dst[r0:r0 + rows, :] = buf[...].astype(jnp.bfloat16)
            depth, last = (in_depth, n_in) if idx < n_in else (out_depth, len(blocks))
            if idx + depth < last:
                fetch(idx + depth).start()
        for ref in (z0, z1, y0, y1, o0, o1, g_s, sg_s, yb_s):
            ref[...] = jnp.zeros(ref.shape, ref.dtype)

    def batch_of(k):
        return lax.div(jnp.clip(k, 0, n_chunks - 1), chunks_per_seq)

    def seq_pos(k):
        return lax.rem(k + chunks_per_seq, chunks_per_seq)

    def iteration(j, par):
        p2 = par % 2
        alt = 1 - p2
        x_rows = slice(par * CHUNK, (par + 1) * CHUNK)
        cur2 = slice(p2 * CHUNK, (p2 + 1) * CHUNK)
        alt2 = slice(alt * CHUNK, (alt + 1) * CHUNK)

        def g_base(offset):
            return ((par + offset) % G_SLOTS) * (SLOT_POS * n_cb)

        vec, mat = [], []

        def stage_a():
            b_a = batch_of(j)
            xv = xa_ref[0, x_rows, :]
            ms = jnp.mean(xv * xv, axis=-1, keepdims=True)
            mult = ng_ref[...] * (1.0 + scale_ref[b_a])
            h = xv * lax.rsqrt(ms + EPS) * mult + shift_ref[b_a]
            h_ring[p2][...] = h.astype(jnp.bfloat16)

        def dot_piece(dst, lhs_ref, w_s, c0):
            def run():
                dst[:, c0:c0 + MXU_COLS] = jnp.dot(lhs_ref[...], w_s[:, c0:c0 + MXU_COLS],
                                                   preferred_element_type=jnp.float32)
            return run
        for c0 in range(0, d, MXU_COLS):
            mat.append((2 * d, dot_piece(o_ring[alt], y_ring[alt], wout_s, c0)))
        for c0 in range(0, 6 * d, MXU_COLS):
            mat.append((d, dot_piece(z_ring[p2], h_ring[p2], win_s, c0)))

        def stage_f():
            b_f = batch_of(j - 4)
            xo = xf_ref[0, x_rows, :] + gate_ref[b_f] * o_ring[p2][...]
            if apply_final_norm:
                ms = jnp.mean(xo * xo, axis=-1, keepdims=True)
                xo = xo * lax.rsqrt(ms + EPS) * fg_ref[...]
            o_ref[0, x_rows, :] = xo
        vec.append((200, stage_f))

        kc = j - 1
        zc = z_ring[alt]

        def glu_piece(s):
            def run():
                main = g_base(-1) + HALO * n_cb
                head_next = g_base(0)
                tail_prev = g_base(-2) + (HALO + CHUNK) * n_cb
                keep_head = seq_pos(kc) != chunks_per_seq - 1
                keep_tail = seq_pos(kc) != 0
                g = zc[:, s * LANES:(s + 1) * LANES] * _sigmoid(zc[:, d + s * LANES:d + (s + 1) * LANES])
                g_s[pl.ds(main + s, CHUNK, stride=n_cb), :] = g
                g_s[pl.ds(head_next + s, HALO, stride=n_cb), :] = jnp.where(
                    keep_head, g[CHUNK - HALO:CHUNK], 0.0)
                g_s[pl.ds(tail_prev + s, HALO, stride=n_cb), :] = jnp.where(
                    keep_tail, g[0:HALO], 0.0)
            return run
        for s in range(n_cb):
            vec.append((30, glu_piece(s)))

        def stage_c_gate():
            sg_s[alt2, :] = _silu(zc[:, 2 * d:3 * d])
        vec.append((160, stage_c_gate))

        def stage_c_gmlp():
            vn = _layer_norm(zc[:, 4 * d:5 * d], slg_ref[...], slb_ref[...]).astype(jnp.bfloat16)
            mixed = jnp.concatenate(
                [jnp.dot(ws_s[hd * CHUNK:(hd + 1) * CHUNK, :],
                         vn[:, hd * HEAD_DIM:(hd + 1) * HEAD_DIM],
                         preferred_element_type=jnp.float32) for hd in range(n_heads)],
                axis=-1) + bs_ref[...]
            yb = zc[:, 3 * d:4 * d] * mixed * _silu(zc[:, 5 * d:6 * d])
            yb_s[alt2, :] = yb.astype(jnp.bfloat16)
        vec.append((500, stage_c_gmlp))

        p = CONV_POS_BLOCK

        def conv_piece(blk):
            def run():
                src = g_base(-2) + (HALO - CONV_HALF) * n_cb
                acc = [cb_ref[...]] * p
                for tap0 in range(0, CONV_WIDTH, CONV_TAP_GROUP):
                    n_taps = min(CONV_TAP_GROUP, CONV_WIDTH - tap0)
                    first = src + (blk * p + tap0) * n_cb
                    window = [g_s[first + e * n_cb:first + (e + 1) * n_cb, :]
                              for e in range(p + n_taps - 1)]
                    for t in range(n_taps):
                        w_tap = cw_ref[tap0 + t]
                        acc = [acc[i] + window[i + t] * w_tap for i in range(p)]
                for i in range(p):
                    co_s[(blk * p + i) * n_cb:(blk * p + i + 1) * n_cb, :] = acc[i]
            return run
        for blk in range(CHUNK // p):
            vec.append((125, conv_piece(blk)))

        def stage_d_tail():
            cv = jnp.concatenate(
                [co_s[pl.ds(s, CHUNK, stride=n_cb), :] for s in range(n_cb)], axis=-1)
            ya = _silu(_layer_norm(cv, clg_ref[...], clb_ref[...])) * sg_s[cur2, :]
            y_ring[p2][:, 0:d] = ya.astype(jnp.bfloat16)
            y_ring[p2][:, d:2 * d] = yb_s[cur2, :]
        vec.append((500, stage_d_tail))

        stage_a()
        for fn in _merged(vec, mat):
            fn()

    for par in range(CHUNKS_PER_STEP):
        pl.when(run_ref[par] == 1)(
            functools.partial(iteration, step * CHUNKS_PER_STEP + par, par))


def _layer_call(x, shift, scale, gate, norm_g, w_in, conv_w, conv_b, conv_ln_g, conv_ln_b,
                sg_ln_g, sg_ln_b, w_s, b_s, w_out, final_g, *, apply_final_norm):
    bsz, seq, d = x.shape
    assert seq % (CHUNK * CHUNKS_PER_STEP) == 0 and PIPE_LAG % CHUNKS_PER_STEP == 0
    assert CHUNKS_PER_STEP % G_SLOTS == 0 and CHUNKS_PER_STEP % 2 == 0
    assert d % LANES == 0 and d // LANES == SUBLANES and w_in.shape == (d, 6 * d)
    assert d % MXU_COLS == 0 and d % CHUNK == 0
    n_heads = w_s.shape[0]
    assert w_s.shape == (n_heads, CHUNK, CHUNK) and n_heads * HEAD_DIM == d
    n_cb = d // LANES
    tile = CHUNK * CHUNKS_PER_STEP
    n_tiles = bsz * seq // tile
    lag_steps = PIPE_LAG // CHUNKS_PER_STEP
    n_steps = n_tiles + lag_steps

    const2 = lambda i, run: (0, 0)
    const3 = lambda i, run: (0, 0, 0)
    load_tile = lambda i, run: (jnp.minimum(i, n_tiles - 1), 0, 0)
    store_tile = lambda i, run: (jnp.maximum(i - lag_steps, 0), 0, 0)
    once = pl.Buffered(1)
    row = lambda v: v.reshape(1, d).astype(jnp.float32)

    bs_full = jnp.repeat(jnp.transpose(b_s), HEAD_DIM, axis=1).astype(jnp.float32)

    in_specs = [
        pl.BlockSpec((1, tile, d), load_tile),
        pl.BlockSpec((1, tile, d), store_tile),
        pl.BlockSpec((bsz, 1, d), const3, pipeline_mode=once),
        pl.BlockSpec((bsz, 1, d), const3, pipeline_mode=once),
        pl.BlockSpec((bsz, 1, d), const3, pipeline_mode=once),
        pl.BlockSpec((1, d), const2, pipeline_mode=once),
        pl.BlockSpec(memory_space=pl.ANY),
        pl.BlockSpec((CONV_WIDTH, n_cb, LANES), const3, pipeline_mode=once),
        pl.BlockSpec((n_cb, LANES), const2, pipeline_mode=once),
        pl.BlockSpec((1, d), const2, pipeline_mode=once),
        pl.BlockSpec((1, d), const2, pipeline_mode=once),
        pl.BlockSpec((1, d), const2, pipeline_mode=once),
        pl.BlockSpec((1, d), const2, pipeline_mode=once),
        pl.BlockSpec((n_heads, CHUNK, CHUNK), const3, pipeline_mode=once),
        pl.BlockSpec((CHUNK, d), const2, pipeline_mode=once),
        pl.BlockSpec(memory_space=pl.ANY),
        pl.BlockSpec((1, d), const2, pipeline_mode=once),
    ]
    scratch_shapes = [
        pltpu.VMEM((d, 6 * d), jnp.bfloat16),
        pltpu.VMEM((2 * d, d), jnp.bfloat16),
        pltpu.VMEM((n_heads * CHUNK, CHUNK), jnp.bfloat16),
        pltpu.SemaphoreType.DMA((W_IN_STREAM_DEPTH + W_OUT_STREAM_DEPTH,)),
        pltpu.VMEM((CHUNK, d), jnp.bfloat16),
        pltpu.VMEM((CHUNK, d), jnp.bfloat16),
        pltpu.VMEM((CHUNK, 6 * d), jnp.float32),
        pltpu.VMEM((CHUNK, 6 * d), jnp.float32),
        pltpu.VMEM((CHUNK, 2 * d), jnp.bfloat16),
        pltpu.VMEM((CHUNK, 2 * d), jnp.bfloat16),
        pltpu.VMEM((CHUNK, d), jnp.float32),
        pltpu.VMEM((CHUNK, d), jnp.float32),
        pltpu.VMEM((G_SLOTS * SLOT_POS * n_cb, LANES), jnp.float32),
        pltpu.VMEM((CHUNK * n_cb, LANES), jnp.float32),
        pltpu.VMEM((2 * CHUNK, d), jnp.float32),
        pltpu.VMEM((2 * CHUNK, d), jnp.bfloat16),
    ]
    out = pl.pallas_call(
        functools.partial(_layer_body, chunks_per_seq=seq // CHUNK,
                          n_chunks=bsz * seq // CHUNK, apply_final_norm=apply_final_norm),
        grid_spec=pltpu.PrefetchScalarGridSpec(
            num_scalar_prefetch=1,
            grid=(n_steps,),
            in_specs=in_specs,
            out_specs=pl.BlockSpec((1, tile, d), store_tile),
            scratch_shapes=scratch_shapes),
        out_shape=jax.ShapeDtypeStruct((n_tiles, tile, d), jnp.float32),
        compiler_params=pltpu.CompilerParams(
            dimension_semantics=("arbitrary",),
            vmem_limit_bytes=VMEM_LIMIT_BYTES),
        name="hybrid_layer",
    )(jnp.ones((CHUNKS_PER_STEP,), jnp.int32),
      x.reshape(n_tiles, tile, d), x.reshape(n_tiles, tile, d), shift, scale, gate,
      row(norm_g), w_in.astype(jnp.float32),
      conv_w.reshape(CONV_WIDTH, n_cb, LANES), conv_b.reshape(n_cb, LANES),
      row(conv_ln_g), row(conv_ln_b), row(sg_ln_g), row(sg_ln_b),
      w_s.astype(jnp.float32), bs_full, w_out.astype(jnp.float32), row(final_g))
    return out.reshape(bsz, seq, d)


def kernel(x, c, w_ada, b_ada, norm_g, w_in, conv_w, conv_b, conv_ln_g, conv_ln_b, sg_ln_g, sg_ln_b, w_s, b_s, w_out, final_g):
    depth = w_in.shape[0]
    bsz, _, d = x.shape
    for l in range(depth):
        mod = _ada_call(c, w_ada[l], b_ada[l])
        shift, scale, gate = (mod[:, k * d:(k + 1) * d].reshape(bsz, 1, d) for k in range(3))
        x = _layer_call(x, shift, scale, gate, norm_g[l], w_in[l], conv_w[l], conv_b[l],
                        conv_ln_g[l], conv_ln_b[l], sg_ln_g[l], sg_ln_b[l], w_s[l], b_s[l],
                        w_out[l], final_g, apply_final_norm=(l == depth - 1))
    return x
```

```python
import functools

import jax
import jax.numpy as jnp
from jax import lax
from jax.experimental import pallas as pl
from jax.experimental.pallas import tpu as pltpu

EPS = 1e-6
CONV_WIDTH = 31
CONV_HALF = CONV_WIDTH // 2
CHUNK = 128
HEAD_DIM = 128
LANES = 128
SUBLANES = 8
HALO = 16
SLOT_POS = CHUNK + 2 * HALO
CHUNKS_PER_STEP = 4
G_SLOTS = 4
PIPE_LAG = 4
CONV_POS_BLOCK = 8
CONV_TAP_GROUP = 8
MXU_COLS = 512
W_IN_STREAM_DEPTH = 8
W_OUT_STREAM_DEPTH = 4
VMEM_LIMIT_BYTES = 56 * 1024 * 1024


def _sigmoid(v):
    return 1.0 / (1.0 + jnp.exp(-v))


def _silu(v):
    return v * _sigmoid(v)


def _layer_norm(v, g, b):
    mu = jnp.mean(v, axis=-1, keepdims=True)
    d = v - mu
    var = jnp.mean(d * d, axis=-1, keepdims=True)
    return d * lax.rsqrt(var + EPS) * g + b


def _ada_body(c_ref, w_ref, b_ref, o_ref):
    c_act = _silu(c_ref[...])
    o_ref[...] = jnp.dot(c_act, w_ref[...], preferred_element_type=jnp.float32,
                         precision=lax.Precision.HIGHEST) + b_ref[...]


def _ada_call(c, w_ada, b_ada):
    bsz, d = c.shape
    n = w_ada.shape[1]
    bn = 512
    return pl.pallas_call(
        _ada_body,
        grid=(n // bn,),
        in_specs=[
            pl.BlockSpec((bsz, d), lambda j: (0, 0)),
            pl.BlockSpec((d, bn), lambda j: (0, j)),
            pl.BlockSpec((1, bn), lambda j: (0, j)),
        ],
        out_specs=pl.BlockSpec((bsz, bn), lambda j: (0, j)),
        out_shape=jax.ShapeDtypeStruct((bsz, n), jnp.float32),
        name="ada_mod",
    )(c, w_ada, b_ada.reshape(1, n))


def _merged(vector_pieces, matrix_pieces):
    total_v = sum(c for c, _ in vector_pieces)
    total_m = sum(c for c, _ in matrix_pieces)
    order = []
    iv = im = 0
    done_v = done_m = 0.0
    while iv < len(vector_pieces) or im < len(matrix_pieces):
        take_m = iv == len(vector_pieces) or (
            im < len(matrix_pieces) and done_m / total_m <= done_v / total_v)
        if take_m:
            cost, fn = matrix_pieces[im]
            im += 1
            done_m += cost
        else:
            cost, fn = vector_pieces[iv]
            iv += 1
            done_v += cost
        order.append(fn)
    return order


def _layer_body(xa_ref, xf_ref, shift_ref, scale_ref, gate_ref, ng_ref,
                win_hbm, cw_ref, cb_ref, clg_ref, clb_ref, slg_ref, slb_ref,
                ws_ref, bs_ref, wout_hbm, fg_ref, o_ref,
                win_s, wout_s, ws_s, stage_sem,
                h0, h1, z0, z1, y0, y1, o0, o1, g_s, co_s, sg_s, yb_s,
                *, chunks_per_seq, n_chunks, apply_final_norm):
    d = xa_ref.shape[2]
    n_cb = d // LANES
    n_heads = d // HEAD_DIM
    h_ring, z_ring, y_ring, o_ring = (h0, h1), (z0, z1), (y0, y1), (o0, o1)
    step = pl.program_id(0)

    @pl.when(step == 0)
    def _():
        in_depth, out_depth = W_IN_STREAM_DEPTH, W_OUT_STREAM_DEPTH
        in_rows, out_rows = 2 * z0.shape[0] // in_depth, o0.shape[0]
        in_stage = [z.at[r0:r0 + in_rows] for z in (z0, z1) for r0 in range(0, z.shape[0], in_rows)]
        out_stage = (o0, o1, sg_s.at[0:out_rows], sg_s.at[out_rows:2 * out_rows])
        blocks = [(win_hbm, win_s, r0, in_rows, in_stage[k % in_depth], k % in_depth)
                  for k, r0 in enumerate(range(0, win_s.shape[0], in_rows))]
        blocks += [(wout_hbm, wout_s, r0, out_rows, out_stage[k % out_depth], in_depth + k % out_depth)
                   for k, r0 in enumerate(range(0, wout_s.shape[0], out_rows))]

        def fetch(idx):
            src, _, r0, rows, buf, sem = blocks[idx]
            return pltpu.make_async_copy(src.at[pl.ds(r0, rows), :], buf, stage_sem.at[sem])

        n_in = win_s.shape[0] // in_rows
        for first, depth in ((0, in_depth), (n_in, out_depth)):
            for k in range(depth):
                fetch(first + k).start()
        for hd in range(n_heads):
            ws_s[hd * CHUNK:(hd + 1) * CHUNK, :] = ws_ref[hd].astype(jnp.bfloat16)
        for idx, (_, dst, r0, rows, buf, _) in enumerate(blocks):
            fetch(idx).wait()
            dst[r0:r0 + rows, :] = buf[...].astype(jnp.bfloat16)
            depth, last = (in_depth, n_in) if idx < n_in else (out_depth, len(blocks))
            if idx + depth < last:
                fetch(idx + depth).start()
        for ref in (z0, z1, y0, y1, o0, o1, g_s, sg_s, yb_s):
            ref[...] = jnp.zeros(ref.shape, ref.dtype)

    def batch_of(k):
        return lax.div(jnp.clip(k, 0, n_chunks - 1), chunks_per_seq)

    def seq_pos(k):
        return lax.rem(k + chunks_per_seq, chunks_per_seq)

    def iteration(j, par):
        p2 = par % 2
        alt = 1 - p2
        x_rows = slice(par * CHUNK, (par + 1) * CHUNK)
        cur2 = slice(p2 * CHUNK, (p2 + 1) * CHUNK)
        alt2 = slice(alt * CHUNK, (alt + 1) * CHUNK)

        def g_base(offset):
            return ((par + offset) % G_SLOTS) * (SLOT_POS * n_cb)

        vec, mat = [], []

        def stage_a():
            b_a = batch_of(j)
            xv = xa_ref[0, x_rows, :]
            ms = jnp.mean(xv * xv, axis=-1, keepdims=True)
            mult = ng_ref[...] * (1.0 + scale_ref[b_a])
            h = xv * lax.rsqrt(ms + EPS) * mult + shift_ref[b_a]
            h_ring[p2][...] = h.astype(jnp.bfloat16)

        def dot_piece(dst, lhs_ref, w_s, c0):
            def run():
                dst[:, c0:c0 + MXU_COLS] = jnp.dot(lhs_ref[...], w_s[:, c0:c0 + MXU_COLS],
                                                   preferred_element_type=jnp.float32)
            return run
        for c0 in range(0, d, MXU_COLS):
            mat.append((2 * d, dot_piece(o_ring[alt], y_ring[alt], wout_s, c0)))
        for c0 in range(0, 6 * d, MXU_COLS):
            mat.append((d, dot_piece(z_ring[p2], h_ring[p2], win_s, c0)))

        def stage_f():
            b_f = batch_of(j - 4)
            xo = xf_ref[0, x_rows, :] + gate_ref[b_f] * o_ring[p2][...]
            if apply_final_norm:
                ms = jnp.mean(xo * xo, axis=-1, keepdims=True)
                xo = xo * lax.rsqrt(ms + EPS) * fg_ref[...]
            o_ref[0, x_rows, :] = xo
        vec.append((200, stage_f))

        kc = j - 1
        zc = z_ring[alt]

        def glu_piece(s):
            def run():
                main = g_base(-1) + HALO * n_cb
                head_next = g_base(0)
                tail_prev = g_base(-2) + (HALO + CHUNK) * n_cb
                keep_head = seq_pos(kc) != chunks_per_seq - 1
                keep_tail = seq_pos(kc) != 0
                g = zc[:, s * LANES:(s + 1) * LANES] * _sigmoid(zc[:, d + s * LANES:d + (s + 1) * LANES])
                g_s[pl.ds(main + s, CHUNK, stride=n_cb), :] = g
                g_s[pl.ds(head_next + s, HALO, stride=n_cb), :] = jnp.where(
                    keep_head, g[CHUNK - HALO:CHUNK], 0.0)
                g_s[pl.ds(tail_prev + s, HALO, stride=n_cb), :] = jnp.where(
                    keep_tail, g[0:HALO], 0.0)
            return run
        for s in range(n_cb):
            vec.append((30, glu_piece(s)))

        def stage_c_gate():
            sg_s[alt2, :] = _silu(zc[:, 2 * d:3 * d])
        vec.append((160, stage_c_gate))

        def stage_c_gmlp():
            vn = _layer_norm(zc[:, 4 * d:5 * d], slg_ref[...], slb_ref[...]).astype(jnp.bfloat16)
            mixed = jnp.concatenate(
                [jnp.dot(ws_s[hd * CHUNK:(hd + 1) * CHUNK, :],
                         vn[:, hd * HEAD_DIM:(hd + 1) * HEAD_DIM],
                         preferred_element_type=jnp.float32) for hd in range(n_heads)],
                axis=-1) + bs_ref[...]
            yb = zc[:, 3 * d:4 * d] * mixed * _silu(zc[:, 5 * d:6 * d])
            yb_s[alt2, :] = yb.astype(jnp.bfloat16)
        vec.append((500, stage_c_gmlp))

        p = CONV_POS_BLOCK

        def conv_piece(blk):
            def run():
                src = g_base(-2) + (HALO - CONV_HALF) * n_cb
                acc = [cb_ref[...]] * p
                for tap0 in range(0, CONV_WIDTH, CONV_TAP_GROUP):
                    n_taps = min(CONV_TAP_GROUP, CONV_WIDTH - tap0)
                    first = src + (blk * p + tap0) * n_cb
                    window = [g_s[first + e * n_cb:first + (e + 1) * n_cb, :]
                              for e in range(p + n_taps - 1)]
                    for t in range(n_taps):
                        w_tap = cw_ref[tap0 + t]
                        acc = [acc[i] + window[i + t] * w_tap for i in range(p)]
                for i in range(p):
                    co_s[(blk * p + i) * n_cb:(blk * p + i + 1) * n_cb, :] = acc[i]
            return run
        for blk in range(CHUNK // p):
            vec.append((125, conv_piece(blk)))

        def stage_d_tail():
            cv = jnp.concatenate(
                [co_s[pl.ds(s, CHUNK, stride=n_cb), :] for s in range(n_cb)], axis=-1)
            ya = _silu(_layer_norm(cv, clg_ref[...], clb_ref[...])) * sg_s[cur2, :]
            y_ring[p2][:, 0:d] = ya.astype(jnp.bfloat16)
            y_ring[p2][:, d:2 * d] = yb_s[cur2, :]
        vec.append((500, stage_d_tail))

        stage_a()
        for fn in _merged(vec, mat):
            fn()

    for par in range(CHUNKS_PER_STEP):
        iteration(step * CHUNKS_PER_STEP + par, par)


def _layer_call(x, shift, scale, gate, norm_g, w_in, conv_w, conv_b, conv_ln_g, conv_ln_b,
                sg_ln_g, sg_ln_b, w_s, b_s, w_out, final_g, *, apply_final_norm):
    bsz, seq, d = x.shape
    assert seq % (CHUNK * CHUNKS_PER_STEP) == 0 and PIPE_LAG % CHUNKS_PER_STEP == 0
    assert CHUNKS_PER_STEP % G_SLOTS == 0 and CHUNKS_PER_STEP % 2 == 0
    assert d % LANES == 0 and d // LANES == SUBLANES and w_in.shape == (d, 6 * d)
    assert d % MXU_COLS == 0 and d % CHUNK == 0
    n_heads = w_s.shape[0]
    assert w_s.shape == (n_heads, CHUNK, CHUNK) and n_heads * HEAD_DIM == d
    n_cb = d // LANES
    tile = CHUNK * CHUNKS_PER_STEP
    n_tiles = bsz * seq // tile
    lag_steps = PIPE_LAG // CHUNKS_PER_STEP
    n_steps = n_tiles + lag_steps

    const2 = lambda i: (0, 0)
    const3 = lambda i: (0, 0, 0)
    load_tile = lambda i: (jnp.minimum(i, n_tiles - 1), 0, 0)
    store_tile = lambda i: (jnp.maximum(i - lag_steps, 0), 0, 0)
    once = pl.Buffered(1)
    row = lambda v: v.reshape(1, d).astype(jnp.float32)

    bs_full = jnp.repeat(jnp.transpose(b_s), HEAD_DIM, axis=1).astype(jnp.float32)

    in_specs = [
        pl.BlockSpec((1, tile, d), load_tile),
        pl.BlockSpec((1, tile, d), store_tile),
        pl.BlockSpec((bsz, 1, d), const3, pipeline_mode=once),
        pl.BlockSpec((bsz, 1, d), const3, pipeline_mode=once),
        pl.BlockSpec((bsz, 1, d), const3, pipeline_mode=once),
        pl.BlockSpec((1, d), const2, pipeline_mode=once),
        pl.BlockSpec(memory_space=pl.ANY),
        pl.BlockSpec((CONV_WIDTH, n_cb, LANES), const3, pipeline_mode=once),
        pl.BlockSpec((n_cb, LANES), const2, pipeline_mode=once),
        pl.BlockSpec((1, d), const2, pipeline_mode=once),
        pl.BlockSpec((1, d), const2, pipeline_mode=once),
        pl.BlockSpec((1, d), const2, pipeline_mode=once),
        pl.BlockSpec((1, d), const2, pipeline_mode=once),
        pl.BlockSpec((n_heads, CHUNK, CHUNK), const3, pipeline_mode=once),
        pl.BlockSpec((CHUNK, d), const2, pipeline_mode=once),
        pl.BlockSpec(memory_space=pl.ANY),
        pl.BlockSpec((1, d), const2, pipeline_mode=once),
    ]
    scratch_shapes = [
        pltpu.VMEM((d, 6 * d), jnp.bfloat16),
        pltpu.VMEM((2 * d, d), jnp.bfloat16),
        pltpu.VMEM((n_heads * CHUNK, CHUNK), jnp.bfloat16),
        pltpu.SemaphoreType.DMA((W_IN_STREAM_DEPTH + W_OUT_STREAM_DEPTH,)),
        pltpu.VMEM((CHUNK, d), jnp.bfloat16),
        pltpu.VMEM((CHUNK, d), jnp.bfloat16),
        pltpu.VMEM((CHUNK, 6 * d), jnp.float32),
        pltpu.VMEM((CHUNK, 6 * d), jnp.float32),
        pltpu.VMEM((CHUNK, 2 * d), jnp.bfloat16),
        pltpu.VMEM((CHUNK, 2 * d), jnp.bfloat16),
        pltpu.VMEM((CHUNK, d), jnp.float32),
        pltpu.VMEM((CHUNK, d), jnp.float32),
        pltpu.VMEM((G_SLOTS * SLOT_POS * n_cb, LANES), jnp.float32),
        pltpu.VMEM((CHUNK * n_cb, LANES), jnp.float32),
        pltpu.VMEM((2 * CHUNK, d), jnp.float32),
        pltpu.VMEM((2 * CHUNK, d), jnp.bfloat16),
    ]
    out = pl.pallas_call(
        functools.partial(_layer_body, chunks_per_seq=seq // CHUNK,
                          n_chunks=bsz * seq // CHUNK, apply_final_norm=apply_final_norm),
        grid=(n_steps,),
        in_specs=in_specs,
        out_specs=pl.BlockSpec((1, tile, d), store_tile),
        scratch_shapes=scratch_shapes,
        out_shape=jax.ShapeDtypeStruct((n_tiles, tile, d), jnp.float32),
        compiler_params=pltpu.CompilerParams(
            dimension_semantics=("arbitrary",),
            vmem_limit_bytes=VMEM_LIMIT_BYTES),
        name="hybrid_layer",
    )(x.reshape(n_tiles, tile, d), x.reshape(n_tiles, tile, d), shift, scale, gate,
      row(norm_g), w_in.astype(jnp.float32),
      conv_w.reshape(CONV_WIDTH, n_cb, LANES), conv_b.reshape(n_cb, LANES),
      row(conv_ln_g), row(conv_ln_b), row(sg_ln_g), row(sg_ln_b),
      w_s.astype(jnp.float32), bs_full, w_out.astype(jnp.float32), row(final_g))
    return out.reshape(bsz, seq, d)


def kernel(x, c, w_ada, b_ada, norm_g, w_in, conv_w, conv_b, conv_ln_g, conv_ln_b, sg_ln_g, sg_ln_b, w_s, b_s, w_out, final_g):
    depth = w_in.shape[0]
    bsz, _, d = x.shape
    for l in range(depth):
        mod = _ada_call(c, w_ada[l], b_ada[l])
        shift, scale, gate = (mod[:, k * d:(k + 1) * d].reshape(bsz, 1, d) for k in range(3))
        x = _layer_call(x, shift, scale, gate, norm_g[l], w_in[l], conv_w[l], conv_b[l],
                        conv_ln_g[l], conv_ln_b[l], sg_ln_g[l], sg_ln_b[l], w_s[l], b_s[l],
                        w_out[l], final_g, apply_final_norm=(l == depth - 1))
    return x
```

```python
import functools

import jax
import jax.numpy as jnp
from jax import lax
from jax.experimental import pallas as pl
from jax.experimental.pallas import tpu as pltpu

EPS = 1e-6
CONV_WIDTH = 31
CONV_HALF = CONV_WIDTH // 2
CHUNK = 128
HEAD_DIM = 128
LANES = 128
SUBLANES = 8
HALO = 16
SLOT_POS = CHUNK + 2 * HALO
CHUNKS_PER_STEP = 4
G_SLOTS = 4
PIPE_LAG = 4
CONV_POS_BLOCK = 8
CONV_TAP_GROUP = 8
MXU_COLS = 512
W_IN_STREAM_DEPTH = 8
W_OUT_STREAM_DEPTH = 4
VMEM_LIMIT_BYTES = 56 * 1024 * 1024


def _sigmoid(v):
    return 1.0 / (1.0 + jnp.exp(-v))


def _silu(v):
    return v * _sigmoid(v)


def _layer_norm(v, g, b):
    mu = jnp.mean(v, axis=-1, keepdims=True)
    d = v - mu
    var = jnp.mean(d * d, axis=-1, keepdims=True)
    return d * lax.rsqrt(var + EPS) * g + b


def _ada_body(c_ref, w_ref, b_ref, o_ref):
    c_act = _silu(c_ref[...])
    o_ref[...] = jnp.dot(c_act, w_ref[...], preferred_element_type=jnp.float32,
                         precision=lax.Precision.HIGHEST) + b_ref[...]


def _ada_call(c, w_ada, b_ada):
    bsz, d = c.shape
    n = w_ada.shape[1]
    bn = 512
    return pl.pallas_call(
        _ada_body,
        grid=(n // bn,),
        in_specs=[
            pl.BlockSpec((bsz, d), lambda j: (0, 0)),
            pl.BlockSpec((d, bn), lambda j: (0, j)),
            pl.BlockSpec((1, bn), lambda j: (0, j)),
        ],
        out_specs=pl.BlockSpec((bsz, bn), lambda j: (0, j)),
        out_shape=jax.ShapeDtypeStruct((bsz, n), jnp.float32),
        name="ada_mod",
    )(c, w_ada, b_ada.reshape(1, n))


def _layer_body(xa_ref, xf_ref, shift_ref, scale_ref, gate_ref, ng_ref,
                win_hbm, cw_ref, cb_ref, clg_ref, clb_ref, slg_ref, slb_ref,
                ws_ref, bs_ref, wout_hbm, fg_ref, o_ref,
                win_s, wout_s, ws_s, stage_sem,
                h0, h1, z0, z1, y0, y1, o0, o1, g_s, co_s, sg_s, yb_s,
                *, chunks_per_seq, n_chunks, apply_final_norm):
    d = xa_ref.shape[2]
    n_cb = d // LANES
    n_heads = d // HEAD_DIM
    h_ring, z_ring, y_ring, o_ring = (h0, h1), (z0, z1), (y0, y1), (o0, o1)
    step = pl.program_id(0)

    @pl.when(step == 0)
    def _():
        in_depth, out_depth = W_IN_STREAM_DEPTH, W_OUT_STREAM_DEPTH
        in_rows, out_rows = 2 * z0.shape[0] // in_depth, o0.shape[0]
        in_stage = [z.at[r0:r0 + in_rows] for z in (z0, z1) for r0 in range(0, z.shape[0], in_rows)]
        out_stage = (o0, o1, sg_s.at[0:out_rows], sg_s.at[out_rows:2 * out_rows])
        blocks = [(win_hbm, win_s, r0, in_rows, in_stage[k % in_depth], k % in_depth)
                  for k, r0 in enumerate(range(0, win_s.shape[0], in_rows))]
        blocks += [(wout_hbm, wout_s, r0, out_rows, out_stage[k % out_depth], in_depth + k % out_depth)
                   for k, r0 in enumerate(range(0, wout_s.shape[0], out_rows))]

        def fetch(idx):
            src, _, r0, rows, buf, sem = blocks[idx]
            return pltpu.make_async_copy(src.at[pl.ds(r0, rows), :], buf, stage_sem.at[sem])

        n_in = win_s.shape[0] // in_rows
        for first, depth in ((0, in_depth), (n_in, out_depth)):
            for k in range(depth):
                fetch(first + k).start()
        for hd in range(n_heads):
            ws_s[hd * CHUNK:(hd + 1) * CHUNK, :] = ws_ref[hd].astype(jnp.bfloat16)
        for idx, (_, dst, r0, rows, buf, _) in enumerate(blocks):
            fetch(idx).wait()
            dst[r0:r0 + rows, :] = buf[...].astype(jnp.bfloat16)
            depth, last = (in_depth, n_in) if idx < n_in else (out_depth, len(blocks))
            if idx + depth < last:
                fetch(idx + depth).start()
        for ref in (z0, z1, y0, y1, o0, o1, g_s, sg_s, yb_s):
            ref[...] = jnp.zeros(ref.shape, ref.dtype)

    def batch_of(k):
        return lax.div(jnp.clip(k, 0, n_chunks - 1), chunks_per_seq)

    def seq_pos(k):
        return lax.rem(k + chunks_per_seq, chunks_per_seq)

    def iteration(j, par):
        p2 = par % 2
        alt = 1 - p2
        x_rows = slice(par * CHUNK, (par + 1) * CHUNK)
        cur2 = slice(p2 * CHUNK, (p2 + 1) * CHUNK)
        alt2 = slice(alt * CHUNK, (alt + 1) * CHUNK)

        def g_base(offset):
            return ((par + offset) % G_SLOTS) * (SLOT_POS * n_cb)

        vec, mat = [], []

        def stage_a():
            b_a = batch_of(j)
            xv = xa_ref[0, x_rows, :]
            ms = jnp.mean(xv * xv, axis=-1, keepdims=True)
            mult = ng_ref[...] * (1.0 + scale_ref[b_a])
            h = xv * lax.rsqrt(ms + EPS) * mult + shift_ref[b_a]
            h_ring[p2][...] = h.astype(jnp.bfloat16)

        def dot_piece(dst, lhs_ref, w_s, c0):
            def run():
                dst[:, c0:c0 + MXU_COLS] = jnp.dot(lhs_ref[...], w_s[:, c0:c0 + MXU_COLS],
                                                   preferred_element_type=jnp.float32)
            return run
        for c0 in range(0, d, MXU_COLS):
            mat.append(dot_piece(o_ring[alt], y_ring[alt], wout_s, c0))
        for c0 in range(0, 6 * d, MXU_COLS):
            mat.append(dot_piece(z_ring[p2], h_ring[p2], win_s, c0))

        def stage_f():
            b_f = batch_of(j - 4)
            xo = xf_ref[0, x_rows, :] + gate_ref[b_f] * o_ring[p2][...]
            if apply_final_norm:
                ms = jnp.mean(xo * xo, axis=-1, keepdims=True)
                xo = xo * lax.rsqrt(ms + EPS) * fg_ref[...]
            o_ref[0, x_rows, :] = xo
        vec.append(stage_f)

        kc = j - 1
        zc = z_ring[alt]

        def glu_piece(s):
            def run():
                main = g_base(-1) + HALO * n_cb
                head_next = g_base(0)
                tail_prev = g_base(-2) + (HALO + CHUNK) * n_cb
                keep_head = seq_pos(kc) != chunks_per_seq - 1
                keep_tail = seq_pos(kc) != 0
                g = zc[:, s * LANES:(s + 1) * LANES] * _sigmoid(zc[:, d + s * LANES:d + (s + 1) * LANES])
                g_s[pl.ds(main + s, CHUNK, stride=n_cb), :] = g
                g_s[pl.ds(head_next + s, HALO, stride=n_cb), :] = jnp.where(
                    keep_head, g[CHUNK - HALO:CHUNK], 0.0)
                g_s[pl.ds(tail_prev + s, HALO, stride=n_cb), :] = jnp.where(
                    keep_tail, g[0:HALO], 0.0)
            return run
        for s in range(n_cb):
            vec.append(glu_piece(s))

        def stage_c_gate():
            sg_s[alt2, :] = _silu(zc[:, 2 * d:3 * d])
        vec.append(stage_c_gate)

        def stage_c_gmlp():
            vn = _layer_norm(zc[:, 4 * d:5 * d], slg_ref[...], slb_ref[...]).astype(jnp.bfloat16)
            mixed = jnp.concatenate(
                [jnp.dot(ws_s[hd * CHUNK:(hd + 1) * CHUNK, :],
                         vn[:, hd * HEAD_DIM:(hd + 1) * HEAD_DIM],
                         preferred_element_type=jnp.float32) for hd in range(n_heads)],
                axis=-1) + bs_ref[...]
            yb = zc[:, 3 * d:4 * d] * mixed * _silu(zc[:, 5 * d:6 * d])
            yb_s[alt2, :] = yb.astype(jnp.bfloat16)
        vec.append(stage_c_gmlp)

        p = CONV_POS_BLOCK

        def conv_piece(blk):
            def run():
                src = g_base(-2) + (HALO - CONV_HALF) * n_cb
                acc = [cb_ref[...]] * p
                for tap0 in range(0, CONV_WIDTH, CONV_TAP_GROUP):
                    n_taps = min(CONV_TAP_GROUP, CONV_WIDTH - tap0)
                    first = src + (blk * p + tap0) * n_cb
                    window = [g_s[first + e * n_cb:first + (e + 1) * n_cb, :]
                              for e in range(p + n_taps - 1)]
                    for t in range(n_taps):
                        w_tap = cw_ref[tap0 + t]
                        acc = [acc[i] + window[i + t] * w_tap for i in range(p)]
                for i in range(p):
                    co_s[(blk * p + i) * n_cb:(blk * p + i + 1) * n_cb, :] = acc[i]
            return run
        for blk in range(CHUNK // p):
            vec.append(conv_piece(blk))

        def stage_d_tail():
            cv = jnp.concatenate(
                [co_s[pl.ds(s, CHUNK, stride=n_cb), :] for s in range(n_cb)], axis=-1)
            ya = _silu(_layer_norm(cv, clg_ref[...], clb_ref[...])) * sg_s[cur2, :]
            y_ring[p2][:, 0:d] = ya.astype(jnp.bfloat16)
            y_ring[p2][:, d:2 * d] = yb_s[cur2, :]
        vec.append(stage_d_tail)

        stage_a()
        for fn in vec + mat:
            fn()

    for par in range(CHUNKS_PER_STEP):
        iteration(step * CHUNKS_PER_STEP + par, par)


def _layer_call(x, shift, scale, gate, norm_g, w_in, conv_w, conv_b, conv_ln_g, conv_ln_b,
                sg_ln_g, sg_ln_b, w_s, b_s, w_out, final_g, *, apply_final_norm):
    bsz, seq, d = x.shape
    assert seq % (CHUNK * CHUNKS_PER_STEP) == 0 and PIPE_LAG % CHUNKS_PER_STEP == 0
    assert CHUNKS_PER_STEP % G_SLOTS == 0 and CHUNKS_PER_STEP % 2 == 0
    assert d % LANES == 0 and d // LANES == SUBLANES and w_in.shape == (d, 6 * d)
    assert d % MXU_COLS == 0 and d % CHUNK == 0
    n_heads = w_s.shape[0]
    assert w_s.shape == (n_heads, CHUNK, CHUNK) and n_heads * HEAD_DIM == d
    n_cb = d // LANES
    tile = CHUNK * CHUNKS_PER_STEP
    n_tiles = bsz * seq // tile
    lag_steps = PIPE_LAG // CHUNKS_PER_STEP
    n_steps = n_tiles + lag_steps

    const2 = lambda i: (0, 0)
    const3 = lambda i: (0, 0, 0)
    load_tile = lambda i: (jnp.minimum(i, n_tiles - 1), 0, 0)
    store_tile = lambda i: (jnp.maximum(i - lag_steps, 0), 0, 0)
    once = pl.Buffered(1)
    row = lambda v: v.reshape(1, d).astype(jnp.float32)

    bs_full = jnp.repeat(jnp.transpose(b_s), HEAD_DIM, axis=1).astype(jnp.float32)

    in_specs = [
        pl.BlockSpec((1, tile, d), load_tile),
        pl.BlockSpec((1, tile, d), store_tile),
        pl.BlockSpec((bsz, 1, d), const3, pipeline_mode=once),
        pl.BlockSpec((bsz, 1, d), const3, pipeline_mode=once),
        pl.BlockSpec((bsz, 1, d), const3, pipeline_mode=once),
        pl.BlockSpec((1, d), const2, pipeline_mode=once),
        pl.BlockSpec(memory_space=pl.ANY),
        pl.BlockSpec((CONV_WIDTH, n_cb, LANES), const3, pipeline_mode=once),
        pl.BlockSpec((n_cb, LANES), const2, pipeline_mode=once),
        pl.BlockSpec((1, d), const2, pipeline_mode=once),
        pl.BlockSpec((1, d), const2, pipeline_mode=once),
        pl.BlockSpec((1, d), const2, pipeline_mode=once),
        pl.BlockSpec((1, d), const2, pipeline_mode=once),
        pl.BlockSpec((n_heads, CHUNK, CHUNK), const3, pipeline_mode=once),
        pl.BlockSpec((CHUNK, d), const2, pipeline_mode=once),
        pl.BlockSpec(memory_space=pl.ANY),
        pl.BlockSpec((1, d), const2, pipeline_mode=once),
    ]
    scratch_shapes = [
        pltpu.VMEM((d, 6 * d), jnp.bfloat16),
        pltpu.VMEM((2 * d, d), jnp.bfloat16),
        pltpu.VMEM((n_heads * CHUNK, CHUNK), jnp.bfloat16),
        pltpu.SemaphoreType.DMA((W_IN_STREAM_DEPTH + W_OUT_STREAM_DEPTH,)),
        pltpu.VMEM((CHUNK, d), jnp.bfloat16),
        pltpu.VMEM((CHUNK, d), jnp.bfloat16),
        pltpu.VMEM((CHUNK, 6 * d), jnp.float32),
        pltpu.VMEM((CHUNK, 6 * d), jnp.float32),
        pltpu.VMEM((CHUNK, 2 * d), jnp.bfloat16),
        pltpu.VMEM((CHUNK, 2 * d), jnp.bfloat16),
        pltpu.VMEM((CHUNK, d), jnp.float32),
        pltpu.VMEM((CHUNK, d), jnp.float32),
        pltpu.VMEM((G_SLOTS * SLOT_POS * n_cb, LANES), jnp.float32),
        pltpu.VMEM((CHUNK * n_cb, LANES), jnp.float32),
        pltpu.VMEM((2 * CHUNK, d), jnp.float32),
        pltpu.VMEM((2 * CHUNK, d), jnp.bfloat16),
    ]
    out = pl.pallas_call(
        functools.partial(_layer_body, chunks_per_seq=seq // CHUNK,
                          n_chunks=bsz * seq // CHUNK, apply_final_norm=apply_final_norm),
        grid=(n_steps,),
        in_specs=in_specs,
        out_specs=pl.BlockSpec((1, tile, d), store_tile),
        scratch_shapes=scratch_shapes,
        out_shape=jax.ShapeDtypeStruct((n_tiles, tile, d), jnp.float32),
        compiler_params=pltpu.CompilerParams(
            dimension_semantics=("arbitrary",),
            vmem_limit_bytes=VMEM_LIMIT_BYTES),
        name="hybrid_layer",
    )(x.reshape(n_tiles, tile, d), x.reshape(n_tiles, tile, d), shift, scale, gate,
      row(norm_g), w_in.astype(jnp.float32),
      conv_w.reshape(CONV_WIDTH, n_cb, LANES), conv_b.reshape(n_cb, LANES),
      row(conv_ln_g), row(conv_ln_b), row(sg_ln_g), row(sg_ln_b),
      w_s.astype(jnp.float32), bs_full, w_out.astype(jnp.float32), row(final_g))
    return out.reshape(bsz, seq, d)


def kernel(x, c, w_ada, b_ada, norm_g, w_in, conv_w, conv_b, conv_ln_g, conv_ln_b, sg_ln_g, sg_ln_b, w_s, b_s, w_out, final_g):
    depth = w_in.shape[0]
    bsz, _, d = x.shape
    for l in range(depth):
        mod = _ada_call(c, w_ada[l], b_ada[l])
        shift, scale, gate = (mod[:, k * d:(k + 1) * d].reshape(bsz, 1, d) for k in range(3))
        x = _layer_call(x, shift, scale, gate, norm_g[l], w_in[l], conv_w[l], conv_b[l],
                        conv_ln_g[l], conv_ln_b[l], sg_ln_g[l], sg_ln_b[l], w_s[l], b_s[l],
                        w_out[l], final_g, apply_final_norm=(l == depth - 1))
    return x
```

```python
import functools

import jax
import jax.numpy as jnp
from jax import lax
from jax.experimental import pallas as pl
from jax.experimental.pallas import tpu as pltpu

EPS = 1e-6
CONV_WIDTH = 31
CONV_HALF = CONV_WIDTH // 2
CHUNK = 128
HEAD_DIM = 128
LANES = 128
SUBLANES = 8
HALO = 16
SLOT_POS = CHUNK + 2 * HALO
CHUNKS_PER_STEP = 4
G_SLOTS = 4
PIPE_LAG = 4
CONV_POS_BLOCK = 8
CONV_TAP_GROUP = 8
MXU_COLS = 1024
W_IN_STREAM_DEPTH = 8
W_OUT_STREAM_DEPTH = 4
VMEM_LIMIT_BYTES = 56 * 1024 * 1024


def _sigmoid(v):
    return 1.0 / (1.0 + jnp.exp(-v))


def _silu(v):
    return v * _sigmoid(v)


def _layer_norm(v, g, b):
    mu = jnp.mean(v, axis=-1, keepdims=True)
    d = v - mu
    var = jnp.mean(d * d, axis=-1, keepdims=True)
    return d * lax.rsqrt(var + EPS) * g + b


def _ada_body(c_ref, w_ref, b_ref, o_ref):
    c_act = _silu(c_ref[...])
    o_ref[...] = jnp.dot(c_act, w_ref[...], preferred_element_type=jnp.float32,
                         precision=lax.Precision.HIGHEST) + b_ref[...]


def _ada_call(c, w_ada, b_ada):
    bsz, d = c.shape
    n = w_ada.shape[1]
    bn = 512
    return pl.pallas_call(
        _ada_body,
        grid=(n // bn,),
        in_specs=[
            pl.BlockSpec((bsz, d), lambda j: (0, 0)),
            pl.BlockSpec((d, bn), lambda j: (0, j)),
            pl.BlockSpec((1, bn), lambda j: (0, j)),
        ],
        out_specs=pl.BlockSpec((bsz, bn), lambda j: (0, j)),
        out_shape=jax.ShapeDtypeStruct((bsz, n), jnp.float32),
        name="ada_mod",
    )(c, w_ada, b_ada.reshape(1, n))


def _layer_body(xa_ref, xf_ref, shift_ref, scale_ref, gate_ref, ng_ref,
                win_hbm, cw_ref, cb_ref, clg_ref, clb_ref, slg_ref, slb_ref,
                ws_ref, bs_ref, wout_hbm, fg_ref, o_ref,
                win_s, wout_s, ws_s, stage_sem,
                h0, h1, z0, z1, y0, y1, o0, o1, g_s, co_s, sg_s, yb_s,
                *, chunks_per_seq, n_chunks, apply_final_norm):
    d = xa_ref.shape[2]
    n_cb = d // LANES
    n_heads = d // HEAD_DIM
    h_ring, z_ring, y_ring, o_ring = (h0, h1), (z0, z1), (y0, y1), (o0, o1)
    step = pl.program_id(0)

    @pl.when(step == 0)
    def _():
        in_depth, out_depth = W_IN_STREAM_DEPTH, W_OUT_STREAM_DEPTH
        in_rows, out_rows = 2 * z0.shape[0] // in_depth, o0.shape[0]
        in_stage = [z.at[r0:r0 + in_rows] for z in (z0, z1) for r0 in range(0, z.shape[0], in_rows)]
        out_stage = (o0, o1, sg_s.at[0:out_rows], sg_s.at[out_rows:2 * out_rows])
        blocks = [(win_hbm, win_s, r0, in_rows, in_stage[k % in_depth], k % in_depth)
                  for k, r0 in enumerate(range(0, win_s.shape[0], in_rows))]
        blocks += [(wout_hbm, wout_s, r0, out_rows, out_stage[k % out_depth], in_depth + k % out_depth)
                   for k, r0 in enumerate(range(0, wout_s.shape[0], out_rows))]

        def fetch(idx):
            src, _, r0, rows, buf, sem = blocks[idx]
            return pltpu.make_async_copy(src.at[pl.ds(r0, rows), :], buf, stage_sem.at[sem])

        n_in = win_s.shape[0] // in_rows
        for first, depth in ((0, in_depth), (n_in, out_depth)):
            for k in range(depth):
                fetch(first + k).start()
        for hd in range(n_heads):
            ws_s[hd * CHUNK:(hd + 1) * CHUNK, :] = ws_ref[hd].astype(jnp.bfloat16)
        for idx, (_, dst, r0, rows, buf, _) in enumerate(blocks):
            fetch(idx).wait()
            dst[r0:r0 + rows, :] = buf[...].astype(jnp.bfloat16)
            depth, last = (in_depth, n_in) if idx < n_in else (out_depth, len(blocks))
            if idx + depth < last:
                fetch(idx + depth).start()
        for ref in (z0, z1, y0, y1, o0, o1, g_s, sg_s, yb_s):
            ref[...] = jnp.zeros(ref.shape, ref.dtype)

    def batch_of(k):
        return lax.div(jnp.clip(k, 0, n_chunks - 1), chunks_per_seq)

    def seq_pos(k):
        return lax.rem(k + chunks_per_seq, chunks_per_seq)

    def iteration(j, par):
        p2 = par % 2
        alt = 1 - p2
        x_rows = slice(par * CHUNK, (par + 1) * CHUNK)
        cur2 = slice(p2 * CHUNK, (p2 + 1) * CHUNK)
        alt2 = slice(alt * CHUNK, (alt + 1) * CHUNK)

        def g_base(offset):
            return ((par + offset) % G_SLOTS) * (SLOT_POS * n_cb)

        vec, mat = [], []

        def stage_a():
            b_a = batch_of(j)
            xv = xa_ref[0, x_rows, :]
            ms = jnp.mean(xv * xv, axis=-1, keepdims=True)
            mult = ng_ref[...] * (1.0 + scale_ref[b_a])
            h = xv * lax.rsqrt(ms + EPS) * mult + shift_ref[b_a]
            h_ring[p2][...] = h.astype(jnp.bfloat16)

        def dot_piece(dst, lhs_ref, w_s, c0):
            def run():
                dst[:, c0:c0 + MXU_COLS] = jnp.dot(lhs_ref[...], w_s[:, c0:c0 + MXU_COLS],
                                                   preferred_element_type=jnp.float32)
            return run
        for c0 in range(0, d, MXU_COLS):
            mat.append(dot_piece(o_ring[alt], y_ring[alt], wout_s, c0))
        for c0 in range(0, 6 * d, MXU_COLS):
            mat.append(dot_piece(z_ring[p2], h_ring[p2], win_s, c0))

        def stage_f():
            b_f = batch_of(j - 4)
            xo = xf_ref[0, x_rows, :] + gate_ref[b_f] * o_ring[p2][...]
            if apply_final_norm:
                ms = jnp.mean(xo * xo, axis=-1, keepdims=True)
                xo = xo * lax.rsqrt(ms + EPS) * fg_ref[...]
            o_ref[0, x_rows, :] = xo
        vec.append(stage_f)

        kc = j - 1
        zc = z_ring[alt]

        def glu_piece(s):
            def run():
                main = g_base(-1) + HALO * n_cb
                head_next = g_base(0)
                tail_prev = g_base(-2) + (HALO + CHUNK) * n_cb
                keep_head = seq_pos(kc) != chunks_per_seq - 1
                keep_tail = seq_pos(kc) != 0
                g = zc[:, s * LANES:(s + 1) * LANES] * _sigmoid(zc[:, d + s * LANES:d + (s + 1) * LANES])
                g_s[pl.ds(main + s, CHUNK, stride=n_cb), :] = g
                g_s[pl.ds(head_next + s, HALO, stride=n_cb), :] = jnp.where(
                    keep_head, g[CHUNK - HALO:CHUNK], 0.0)
                g_s[pl.ds(tail_prev + s, HALO, stride=n_cb), :] = jnp.where(
                    keep_tail, g[0:HALO], 0.0)
            return run
        for s in range(n_cb):
            vec.append(glu_piece(s))

        def stage_c_gate():
            sg_s[alt2, :] = _silu(zc[:, 2 * d:3 * d])
        vec.append(stage_c_gate)

        def stage_c_gmlp():
            vn = _layer_norm(zc[:, 4 * d:5 * d], slg_ref[...], slb_ref[...]).astype(jnp.bfloat16)
            mixed = jnp.concatenate(
                [jnp.dot(ws_s[hd * CHUNK:(hd + 1) * CHUNK, :],
                         vn[:, hd * HEAD_DIM:(hd + 1) * HEAD_DIM],
                         preferred_element_type=jnp.float32) for hd in range(n_heads)],
                axis=-1) + bs_ref[...]
            yb = zc[:, 3 * d:4 * d] * mixed * _silu(zc[:, 5 * d:6 * d])
            yb_s[alt2, :] = yb.astype(jnp.bfloat16)
        vec.append(stage_c_gmlp)

        p = CONV_POS_BLOCK

        def conv_piece(blk):
            def run():
                src = g_base(-2) + (HALO - CONV_HALF) * n_cb
                acc = [cb_ref[...]] * p
                for tap0 in range(0, CONV_WIDTH, CONV_TAP_GROUP):
                    n_taps = min(CONV_TAP_GROUP, CONV_WIDTH - tap0)
                    first = src + (blk * p + tap0) * n_cb
                    window = [g_s[first + e * n_cb:first + (e + 1) * n_cb, :]
                              for e in range(p + n_taps - 1)]
                    for t in range(n_taps):
                        w_tap = cw_ref[tap0 + t]
                        acc = [acc[i] + window[i + t] * w_tap for i in range(p)]
                for i in range(p):
                    co_s[(blk * p + i) * n_cb:(blk * p + i + 1) * n_cb, :] = acc[i]
            return run
        for blk in range(CHUNK // p):
            vec.append(conv_piece(blk))

        def stage_d_tail():
            cv = jnp.concatenate(
                [co_s[pl.ds(s, CHUNK, stride=n_cb), :] for s in range(n_cb)], axis=-1)
            ya = _silu(_layer_norm(cv, clg_ref[...], clb_ref[...])) * sg_s[cur2, :]
            y_ring[p2][:, 0:d] = ya.astype(jnp.bfloat16)
            y_ring[p2][:, d:2 * d] = yb_s[cur2, :]
        vec.append(stage_d_tail)

        stage_a()
        for fn in vec + mat:
            fn()

    for par in range(CHUNKS_PER_STEP):
        iteration(step * CHUNKS_PER_STEP + par, par)


def _layer_call(x, shift, scale, gate, norm_g, w_in, conv_w, conv_b, conv_ln_g, conv_ln_b,
                sg_ln_g, sg_ln_b, w_s, b_s, w_out, final_g, *, apply_final_norm):
    bsz, seq, d = x.shape
    assert seq % (CHUNK * CHUNKS_PER_STEP) == 0 and PIPE_LAG % CHUNKS_PER_STEP == 0
    assert CHUNKS_PER_STEP % G_SLOTS == 0 and CHUNKS_PER_STEP % 2 == 0
    assert d % LANES == 0 and d // LANES == SUBLANES and w_in.shape == (d, 6 * d)
    assert d % MXU_COLS == 0 and d % CHUNK == 0
    n_heads = w_s.shape[0]
    assert w_s.shape == (n_heads, CHUNK, CHUNK) and n_heads * HEAD_DIM == d
    n_cb = d // LANES
    tile = CHUNK * CHUNKS_PER_STEP
    n_tiles = bsz * seq // tile
    lag_steps = PIPE_LAG // CHUNKS_PER_STEP
    n_steps = n_tiles + lag_steps

    const2 = lambda i: (0, 0)
    const3 = lambda i: (0, 0, 0)
    load_tile = lambda i: (jnp.minimum(i, n_tiles - 1), 0, 0)
    store_tile = lambda i: (jnp.maximum(i - lag_steps, 0), 0, 0)
    once = pl.Buffered(1)
    row = lambda v: v.reshape(1, d).astype(jnp.float32)

    bs_full = jnp.repeat(jnp.transpose(b_s), HEAD_DIM, axis=1).astype(jnp.float32)

    in_specs = [
        pl.BlockSpec((1, tile, d), load_tile),
        pl.BlockSpec((1, tile, d), store_tile),
        pl.BlockSpec((bsz, 1, d), const3, pipeline_mode=once),
        pl.BlockSpec((bsz, 1, d), const3, pipeline_mode=once),
        pl.BlockSpec((bsz, 1, d), const3, pipeline_mode=once),
        pl.BlockSpec((1, d), const2, pipeline_mode=once),
        pl.BlockSpec(memory_space=pl.ANY),
        pl.BlockSpec((CONV_WIDTH, n_cb, LANES), const3, pipeline_mode=once),
        pl.BlockSpec((n_cb, LANES), const2, pipeline_mode=once),
        pl.BlockSpec((1, d), const2, pipeline_mode=once),
        pl.BlockSpec((1, d), const2, pipeline_mode=once),
        pl.BlockSpec((1, d), const2, pipeline_mode=once),
        pl.BlockSpec((1, d), const2, pipeline_mode=once),
        pl.BlockSpec((n_heads, CHUNK, CHUNK), const3, pipeline_mode=once),
        pl.BlockSpec((CHUNK, d), const2, pipeline_mode=once),
        pl.BlockSpec(memory_space=pl.ANY),
        pl.BlockSpec((1, d), const2, pipeline_mode=once),
    ]
    scratch_shapes = [
        pltpu.VMEM((d, 6 * d), jnp.bfloat16),
        pltpu.VMEM((2 * d, d), jnp.bfloat16),
        pltpu.VMEM((n_heads * CHUNK, CHUNK), jnp.bfloat16),
        pltpu.SemaphoreType.DMA((W_IN_STREAM_DEPTH + W_OUT_STREAM_DEPTH,)),
        pltpu.VMEM((CHUNK, d), jnp.bfloat16),
        pltpu.VMEM((CHUNK, d), jnp.bfloat16),
        pltpu.VMEM((CHUNK, 6 * d), jnp.float32),
        pltpu.VMEM((CHUNK, 6 * d), jnp.float32),
        pltpu.VMEM((CHUNK, 2 * d), jnp.bfloat16),
        pltpu.VMEM((CHUNK, 2 * d), jnp.bfloat16),
        pltpu.VMEM((CHUNK, d), jnp.float32),
        pltpu.VMEM((CHUNK, d), jnp.float32),
        pltpu.VMEM((G_SLOTS * SLOT_POS * n_cb, LANES), jnp.float32),
        pltpu.VMEM((CHUNK * n_cb, LANES), jnp.float32),
        pltpu.VMEM((2 * CHUNK, d), jnp.float32),
        pltpu.VMEM((2 * CHUNK, d), jnp.bfloat16),
    ]
    out = pl.pallas_call(
        functools.partial(_layer_body, chunks_per_seq=seq // CHUNK,
                          n_chunks=bsz * seq // CHUNK, apply_final_norm=apply_final_norm),
        grid=(n_steps,),
        in_specs=in_specs,
        out_specs=pl.BlockSpec((1, tile, d), store_tile),
        scratch_shapes=scratch_shapes,
        out_shape=jax.ShapeDtypeStruct((n_tiles, tile, d), jnp.float32),
        compiler_params=pltpu.CompilerParams(
            dimension_semantics=("arbitrary",),
            vmem_limit_bytes=VMEM_LIMIT_BYTES),
        name="hybrid_layer",
    )(x.reshape(n_tiles, tile, d), x.reshape(n_tiles, tile, d), shift, scale, gate,
      row(norm_g), w_in.astype(jnp.float32),
      conv_w.reshape(CONV_WIDTH, n_cb, LANES), conv_b.reshape(n_cb, LANES),
      row(conv_ln_g), row(conv_ln_b), row(sg_ln_g), row(sg_ln_b),
      w_s.astype(jnp.float32), bs_full, w_out.astype(jnp.float32), row(final_g))
    return out.reshape(bsz, seq, d)


def kernel(x, c, w_ada, b_ada, norm_g, w_in, conv_w, conv_b, conv_ln_g, conv_ln_b, sg_ln_g, sg_ln_b, w_s, b_s, w_out, final_g):
    depth = w_in.shape[0]
    bsz, _, d = x.shape
    for l in range(depth):
        mod = _ada_call(c, w_ada[l], b_ada[l])
        shift, scale, gate = (mod[:, k * d:(k + 1) * d].reshape(bsz, 1, d) for k in range(3))
        x = _layer_call(x, shift, scale, gate, norm_g[l], w_in[l], conv_w[l], conv_b[l],
                        conv_ln_g[l], conv_ln_b[l], sg_ln_g[l], sg_ln_b[l], w_s[l], b_s[l],
                        w_out[l], final_g, apply_final_norm=(l == depth - 1))
    return x
```
